```python
import math
import jax, jax.numpy as jnp
from jax import lax
import numpy as np

D_MODEL = 2048
BATCH = 4
SEQ = 4096
DEPTH = 2

N_MIXERS = 2
RMS_EPS = 1e-6
RWKV_HEAD = 64
RWKV_HEADS = D_MODEL // RWKV_HEAD
DECAY_LORA = max(32, int(round(1.8 * D_MODEL ** 0.5 / 32)) * 32)
ICLR_LORA = max(32, int(round(1.8 * D_MODEL ** 0.5 / 32)) * 32)
GATE_LORA = max(32, int(round(0.6 * D_MODEL ** 0.8 / 32)) * 32)
GN_EPS = RWKV_HEAD * 1e-5
N_SHIFT_MIX = 6
DIFF_HEADS = 16
DIFF_HEAD = D_MODEL // (2 * DIFF_HEADS)
Q_BLOCK = 128
ROPE_THETA = 10000.0
SUBLN_EPS = 1e-5
FFN_DIM = 7 * D_MODEL // 2
N_EXPERTS = 8
TOP_K = 2
MOE_BLOCK = 128
N_EVEN = (DEPTH + 1) // 2
N_ODD = DEPTH // 2

kernel_name = "bidir_rwkv7_diffattn_moe_hybrid"


def rms_norm(x, g, eps=RMS_EPS):
    xf = x.astype(jnp.float32)
    return (xf * lax.rsqrt(jnp.mean(xf * xf, -1, keepdims=True) + eps) * g).astype(x.dtype)


def centred_shift(x):
    xp = jnp.pad(x, ((0, 0), (1, 1), (0, 0)))
    return 0.5 * (xp[:, :-2] + xp[:, 2:])


def wkv7_scan(r, w, k, v, a, b, reverse):
    B, S, H, N = r.shape
    seq = tuple(jnp.moveaxis(t, 1, 0) for t in (r, w, k, v, a, b))

    def step(state, inp):
        r_t, w_t, k_t, v_t, a_t, b_t = inp
        sa = jnp.einsum('bhvk,bhk->bhv', state, a_t)
        state = (state * w_t[:, :, None, :] + sa[..., None] * b_t[:, :, None, :]
                 + v_t[..., None] * k_t[:, :, None, :])
        return state, jnp.einsum('bhvk,bhk->bhv', state, r_t)

    s0 = jnp.zeros((B, H, N, N), jnp.float32)
    _, y = lax.scan(step, s0, seq, reverse=reverse)
    return jnp.moveaxis(y, 0, 1)


def rwkv7_bidir(x, mu, w_rkv, g1, g2, w0, w1, w2, a0, a1, a2, k_k, k_a, r_k, ln_w, ln_b, w_o):
    B, S, D = x.shape
    H, N = RWKV_HEADS, RWKV_HEAD
    heads = lambda t: t.reshape(B, S, H, N)
    xm = x[:, :, None, :] + (centred_shift(x) - x)[:, :, None, :] * mu
    rkv = jnp.einsum('bsnd,dne->bsne', xm[:, :, :3], w_rkv.reshape(D, 3, D))
    r, k, v = rkv[:, :, 0], rkv[:, :, 1], rkv[:, :, 2]
    xw, xa, xg = xm[:, :, 3], xm[:, :, 4], xm[:, :, 5]
    g = jax.nn.sigmoid(xg @ g1) @ g2
    kk = heads(k * k_k).astype(jnp.float32)
    kk = kk / jnp.maximum(jnp.sqrt(jnp.sum(kk * kk, -1, keepdims=True)), 1e-12)
    rh = heads(r).astype(jnp.float32)
    vh = heads(v).astype(jnp.float32)
    rkh = r_k.reshape(H, N).astype(jnp.float32)
    ys, bonuses = [], []
    for d in range(2):
        logw = -jax.nn.softplus(-(w0[d] + jnp.tanh(xw @ w1[d]) @ w2[d]).astype(jnp.float32)) - 0.5
        decay = heads(jnp.exp(-jnp.exp(logw)))
        a = jax.nn.sigmoid((a0[d] + (xa @ a1[d]) @ a2[d]).astype(jnp.float32))
        kd = heads(k.astype(jnp.float32) * (1.0 + (a - 1.0) * k_a))
        ah = heads(a)
        ys.append(wkv7_scan(rh, decay, kd, vh, -kk, kk * ah, reverse=(d == 1)))
        bonuses.append(jnp.sum(rh * kd * rkh, -1, keepdims=True) * vh)
    y = ys[0] + ys[1]
    mean = jnp.mean(y, -1, keepdims=True)
    var = jnp.mean(jnp.square(y - mean), -1, keepdims=True)
    yn = ((y - mean) * lax.rsqrt(var + GN_EPS)).reshape(B, S, D) * ln_w + ln_b
    out = yn + (bonuses[0] + bonuses[1]).reshape(B, S, D)
    return (out.astype(x.dtype) * g) @ w_o


def rope_cos_sin(positions, dim):
    inv = ROPE_THETA ** (-jnp.arange(0, dim, 2, dtype=jnp.float32) / dim)
    ang = positions.astype(jnp.float32)[..., None] * inv
    return jnp.cos(ang), jnp.sin(ang)


def apply_rope(t, cos, sin):
    c = cos[:, :, None, None, :]
    s = sin[:, :, None, None, :]
    tf = t.astype(jnp.float32)
    t1, t2 = jnp.split(tf, 2, axis=-1)
    return jnp.concatenate([t1 * c - t2 * s, t2 * c + t1 * s], -1).astype(t.dtype)


def diff_attention(x, positions, w_qkv, lq1, lk1, lq2, lk2, subln, w_o, lambda_init):
    B, S, D = x.shape
    H, Dh = DIFF_HEADS, DIFF_HEAD
    q, k, v = jnp.split(x @ w_qkv, 3, axis=-1)
    q = q.reshape(B, S, H, 2, Dh)
    k = k.reshape(B, S, H, 2, Dh)
    v = v.reshape(B, S, H, 2 * Dh)
    cos, sin = rope_cos_sin(positions, Dh)
    q = apply_rope(q, cos, sin) * (Dh ** -0.5)
    k = apply_rope(k, cos, sin)
    lam = (jnp.exp(jnp.sum(lq1.astype(jnp.float32) * lk1)) -
           jnp.exp(jnp.sum(lq2.astype(jnp.float32) * lk2)) + lambda_init)
    nb = S // Q_BLOCK
    qb = q.reshape(B, nb, Q_BLOCK, H, 2, Dh).swapaxes(0, 1)

    def block(q_blk):
        s = jnp.einsum('bqhcd,bkhcd->bhcqk', q_blk, k).astype(jnp.float32)
        p = jax.nn.softmax(s, axis=-1)
        att = p[:, :, 0] - lam * p[:, :, 1]
        return jnp.einsum('bhqk,bkhe->bqhe', att.astype(v.dtype), v)

    o = lax.map(block, qb).swapaxes(0, 1).reshape(B, S, H, 2 * Dh).astype(jnp.float32)
    o = o * lax.rsqrt(jnp.mean(o * o, -1, keepdims=True) + SUBLN_EPS) * subln * (1.0 - lambda_init)
    return o.reshape(B, S, D).astype(x.dtype) @ w_o


def swiglu(x, wg, wu, wd):
    return (jax.nn.silu(x @ wg) * (x @ wu)) @ wd


def moe_swiglu(x, w_router, wg, wu, wd):
    B, S, D = x.shape
    T = B * S
    xf = x.reshape(T, D)
    logits = (xf @ w_router).astype(jnp.float32)
    top_val, top_idx = lax.top_k(logits, TOP_K)
    gates = jax.nn.softmax(top_val, axis=-1)
    flat_e = top_idx.reshape(-1).astype(jnp.int32)
    flat_t = jnp.repeat(jnp.arange(T, dtype=jnp.int32), TOP_K)
    flat_g = gates.reshape(-1)
    order = jnp.argsort(flat_e)
    se = flat_e[order]
    counts = jax.ops.segment_sum(jnp.ones_like(flat_e), flat_e, num_segments=N_EXPERTS)
    starts = jnp.cumsum(counts) - counts
    pcounts = (counts + MOE_BLOCK - 1) // MOE_BLOCK * MOE_BLOCK
    pends = jnp.cumsum(pcounts)
    pstarts = pends - pcounts
    dest = pstarts[se] + (jnp.arange(T * TOP_K, dtype=jnp.int32) - starts[se])
    P = T * TOP_K + N_EXPERTS * MOE_BLOCK
    nblk = P // MOE_BLOCK
    buf_t = jnp.full((P,), T, jnp.int32).at[dest].set(flat_t[order])
    buf_g = jnp.zeros((P,), jnp.float32).at[dest].set(flat_g[order])
    blk_e = jnp.minimum(jnp.searchsorted(pends, jnp.arange(nblk, dtype=jnp.int32) * MOE_BLOCK, side='right'),
                        N_EXPERTS - 1)
    x_pad = jnp.concatenate([xf, jnp.zeros((1, D), xf.dtype)], axis=0)
    xb = x_pad[buf_t].reshape(nblk, MOE_BLOCK, D)

    def expert_block(args):
        xe, e = args
        return swiglu(xe, wg[e], wu[e], wd[e])

    yb = lax.map(expert_block, (xb, blk_e)).reshape(P, D)
    y = jnp.zeros((T + 1, D), x.dtype).at[buf_t].add(yb * buf_g[:, None].astype(yb.dtype))
    return y[:T].reshape(B, S, D)


def setup_inputs(seed: int = 0) -> dict:
    key = jax.random.key(seed)
    ks = iter(jax.random.split(key, 48))
    D, F, E = D_MODEL, FFN_DIM, N_EXPERTS
    f32 = jnp.float32
    nrm = lambda shape, scale: jax.random.normal(next(ks), shape, f32) * scale
    uni = lambda shape, lo, hi: jax.random.uniform(next(ks), shape, f32, lo, hi)
    return {
        'x': jax.random.normal(next(ks), (BATCH, SEQ, D), f32),
        'positions': jnp.broadcast_to(jnp.arange(SEQ, dtype=jnp.int32), (BATCH, SEQ)),
        'norm_mix': 1.0 + nrm((DEPTH, D), 0.02),
        'norm_ffn': 1.0 + nrm((DEPTH, D), 0.02),
        'norm_final': 1.0 + nrm((D,), 0.02),
        'rw_mu': uni((N_EVEN, N_SHIFT_MIX, D), 0.0, 1.0),
        'rw_w_rkv': nrm((N_EVEN, D, 3 * D), D ** -0.5),
        'rw_g1': nrm((N_EVEN, D, GATE_LORA), D ** -0.5),
        'rw_g2': nrm((N_EVEN, GATE_LORA, D), GATE_LORA ** -0.5),
        'rw_w0': uni((N_EVEN, 2, D), -6.5, -1.5),
        'rw_w1': nrm((N_EVEN, 2, D, DECAY_LORA), D ** -0.5),
        'rw_w2': nrm((N_EVEN, 2, DECAY_LORA, D), DECAY_LORA ** -0.5),
        'rw_a0': nrm((N_EVEN, 2, D), 0.1),
        'rw_a1': nrm((N_EVEN, 2, D, ICLR_LORA), D ** -0.5),
        'rw_a2': nrm((N_EVEN, 2, ICLR_LORA, D), ICLR_LORA ** -0.5),
        'rw_kk': 0.85 + nrm((N_EVEN, D), 0.05),
        'rw_ka': 1.0 + nrm((N_EVEN, D), 0.05),
        'rw_rk': nrm((N_EVEN, D), 0.1),
        'rw_ln_w': 1.0 + nrm((N_EVEN, D), 0.02),
        'rw_ln_b': nrm((N_EVEN, D), 0.02),
        'rw_w_o': nrm((N_EVEN, D, D), D ** -0.5),
        'da_w_qkv': nrm((N_ODD, D, 3 * D), D ** -0.5),
        'da_lq1': nrm((N_ODD, DIFF_HEAD), 0.1),
        'da_lk1': nrm((N_ODD, DIFF_HEAD), 0.1),
        'da_lq2': nrm((N_ODD, DIFF_HEAD), 0.1),
        'da_lk2': nrm((N_ODD, DIFF_HEAD), 0.1),
        'da_subln': 1.0 + nrm((N_ODD, 2 * DIFF_HEAD), 0.02),
        'da_w_o': nrm((N_ODD, D, D), D ** -0.5),
        'ff_wg': nrm((N_EVEN, D, F), D ** -0.5),
        'ff_wu': nrm((N_EVEN, D, F), D ** -0.5),
        'ff_wd': nrm((N_EVEN, F, D), F ** -0.5),
        'moe_router': nrm((N_ODD, D, E), D ** -0.5),
        'moe_wg': nrm((N_ODD, E, D, F), D ** -0.5),
        'moe_wu': nrm((N_ODD, E, D, F), D ** -0.5),
        'moe_wd': nrm((N_ODD, E, F, D), F ** -0.5),
    }


def reference(x, positions, norm_mix, norm_ffn, norm_final,
              rw_mu, rw_w_rkv, rw_g1, rw_g2, rw_w0, rw_w1, rw_w2, rw_a0, rw_a1, rw_a2,
              rw_kk, rw_ka, rw_rk, rw_ln_w, rw_ln_b, rw_w_o,
              da_w_qkv, da_lq1, da_lk1, da_lq2, da_lk2, da_subln, da_w_o,
              ff_wg, ff_wu, ff_wd,
              moe_router, moe_wg, moe_wu, moe_wd):
    h = x
    for i in range(DEPTH):
        j = i // N_MIXERS
        hn = rms_norm(h, norm_mix[i])
        if i % N_MIXERS == 0:
            h = h + rwkv7_bidir(hn, rw_mu[j], rw_w_rkv[j], rw_g1[j], rw_g2[j], rw_w0[j], rw_w1[j], rw_w2[j],
                                rw_a0[j], rw_a1[j], rw_a2[j], rw_kk[j], rw_ka[j], rw_rk[j],
                                rw_ln_w[j], rw_ln_b[j], rw_w_o[j])
        else:
            lambda_init = 0.8 - 0.6 * math.exp(-0.3 * i)
            h = h + diff_attention(hn, positions, da_w_qkv[j], da_lq1[j], da_lk1[j], da_lq2[j], da_lk2[j],
                                   da_subln[j], da_w_o[j], lambda_init)
        hn = rms_norm(h, norm_ffn[i])
        if i % 2 == 0:
            h = h + swiglu(hn, ff_wg[j], ff_wu[j], ff_wd[j])
        else:
            h = h + moe_swiglu(hn, moe_router[j], moe_wg[j], moe_wu[j], moe_wd[j])
    return rms_norm(h, norm_final)
```

```python
import functools
import math

import jax
import jax.numpy as jnp
from jax import lax
from jax.experimental import pallas as pl
from jax.experimental.pallas import tpu as pltpu

F32 = jnp.float32
BF16 = jnp.bfloat16

RMS_EPS = 1e-6
HEAD = 64
GN_EPS = HEAD * 1e-5
ROPE_THETA = 10000.0
SUBLN_EPS = 1e-5
TOP_K = 2
LANES = 128
CHUNK = 64
GROUP = 4
GW = GROUP * HEAD
VMEM_LIMIT = 56 * 1024 * 1024


def _cparams(sem, vmem=VMEM_LIMIT):
    return pltpu.CompilerParams(dimension_semantics=sem, vmem_limit_bytes=vmem)


def _dot(a, b):
    return jnp.dot(a.astype(BF16), b.astype(BF16), preferred_element_type=F32)


def _dot_f32(a, b):
    return jnp.dot(a, b, preferred_element_type=F32, precision=lax.Precision.HIGHEST)


def _dot_nt(a, b):
    return lax.dot_general(a.astype(BF16), b.astype(BF16), (((1,), (1,)), ((), ())),
                           preferred_element_type=F32)


def _dot_tn(a, b):
    return lax.dot_general(a.astype(BF16), b.astype(BF16), (((0,), (0,)), ((), ())),
                           preferred_element_type=F32)


def _mm_kernel(*refs, has_bias, has_res):
    a_ref, b_ref = refs[0], refs[1]
    o_ref = refs[-1]
    acc = _dot(a_ref[...], b_ref[...])
    pos = 2
    if has_bias:
        acc = acc + refs[pos][...]
        pos += 1
    if has_res:
        acc = acc + refs[pos][...]
    o_ref[...] = acc.astype(o_ref.dtype)


def matmul(a, b, *, n=None, col_off=0, bias=None, res=None, out_dtype=F32, tm=512, tn=512):
    m, k = a.shape
    n = b.shape[1] if n is None else n
    tm, tn = min(tm, m), min(tn, n)
    assert m % tm == 0 and n % tn == 0 and col_off % tn == 0
    off = col_off // tn
    in_specs = [pl.BlockSpec((tm, k), lambda i, j: (i, 0)),
                pl.BlockSpec((k, tn), lambda i, j: (0, j + off))]
    args = [a, b]
    if bias is not None:
        in_specs.append(pl.BlockSpec((1, tn), lambda i, j: (0, j)))
        args.append(bias.reshape(1, n))
    if res is not None:
        in_specs.append(pl.BlockSpec((tm, tn), lambda i, j: (i, j)))
        args.append(res)
    return pl.pallas_call(
        functools.partial(_mm_kernel, has_bias=bias is not None, has_res=res is not None),
        grid=(m // tm, n // tn),
        in_specs=in_specs,
        out_specs=pl.BlockSpec((tm, tn), lambda i, j: (i, j)),
        out_shape=jax.ShapeDtypeStruct((m, n), out_dtype),
        compiler_params=_cparams(("parallel", "parallel")),
        name="matmul",
    )(*args)


def _wkv_kernel(r_ref, k_ref, v_ref, wz_ref, az_ref, kk_ref, ka_ref, rk_ref,
                y_ref, bonus_ref, state_ref, *, n_groups):
    L = CHUNK
    d = pl.program_id(0)
    sgn = 1 - 2 * d

    @pl.when(pl.program_id(3) == 0)
    def _():
        state_ref[...] = jnp.zeros_like(state_ref)

    row = lax.broadcasted_iota(jnp.int32, (L, GW), 0)
    col = lax.broadcasted_iota(jnp.int32, (L, GW), 1) % L
    diff = (row - col) * sgn
    strict = diff > 0
    incl = diff >= 0
    eye = diff == 0
    level_masks = []
    s = 1
    while s < L:
        level_masks.append(strict & ((row // (2 * s)) == (col // (2 * s))) & ((row // s) != (col // s)))
        s *= 2
    bd_mask = (lax.broadcasted_iota(jnp.int32, (GROUP * L, GW), 0) // L
               == lax.broadcasted_iota(jnp.int32, (GROUP * L, GW), 1) // HEAD)
    seg_ones = bd_mask.astype(BF16)
    tri = (lax.broadcasted_iota(jnp.int32, (L, L), 0) - lax.broadcasted_iota(jnp.int32, (L, L), 1)) * sgn >= 0
    tri = tri.astype(F32)

    def bd(x):
        xb = x.astype(BF16)
        return jnp.where(bd_mask, jnp.concatenate([xb] * GROUP, axis=0), jnp.zeros((), BF16))

    def bdmm(x, y):
        return jnp.dot(x.astype(BF16), bd(y), preferred_element_type=F32)

    def fold(full):
        fm = jnp.where(bd_mask, full, 0.0)
        out = fm[0:HEAD]
        for j in range(1, GROUP):
            out = out + fm[j * HEAD:(j + 1) * HEAD]
        return out

    ys, bonuses = [], []
    for g in range(n_groups):
        sl = slice(g * GW, (g + 1) * GW)
        r = r_ref[0, :, sl]
        k = k_ref[0, :, sl]
        v = v_ref[0, :, sl]
        wz = wz_ref[0, 0, :, sl]
        az = az_ref[0, 0, :, sl]
        k_k = kk_ref[:, sl]
        k_a = ka_ref[:, sl]
        r_k = rk_ref[:, sl]

        u = -wz
        softplus = jnp.maximum(u, 0.0) + jnp.log(1.0 + jnp.exp(-jnp.abs(u)))
        lw = -jnp.exp(-softplus - 0.5)
        a_s = 1.0 / (1.0 + jnp.exp(-az))
        kkr = k * k_k
        ss = jnp.dot((kkr * kkr).astype(BF16), seg_ones, preferred_element_type=F32)
        kk = kkr / jnp.maximum(jnp.sqrt(ss), 1e-12)
        kd = k * (1.0 + (a_s - 1.0) * k_a)
        b_vec = kk * a_s
        rkd = jnp.dot((r * kd * r_k).astype(BF16), seg_ones, preferred_element_type=F32)
        bonuses.append(rkd * v)

        c = _dot_f32(tri, lw)
        tot = jnp.sum(lw, axis=0, keepdims=True)
        e_nc = jnp.exp(-c)
        e_tc = jnp.exp(tot - c)
        at = -kk * jnp.exp(c - lw)
        rt = r * jnp.exp(c)
        bt = b_vec * e_nc
        kt = kd * e_nc
        bh = b_vec * e_tc
        kh = kd * e_tc

        ar = jnp.concatenate([at, rt], axis=0)
        pb = _dot_nt(ar, bd(bt))
        pk = _dot_nt(ar, bd(kt))
        a_ab = jnp.where(strict, pb[:L], 0.0)
        a_rb = jnp.where(incl, pb[L:], 0.0)
        a_ak = jnp.where(strict, pk[:L], 0.0)
        a_rk = jnp.where(incl, pk[L:], 0.0)

        t_inv = jnp.where(eye, 1.0, 0.0) + jnp.where(level_masks[0], a_ab, 0.0)
        for m in level_masks[1:]:
            t1 = bdmm(t_inv, jnp.where(m, a_ab, 0.0))
            t_inv = t_inv + bdmm(t1, t_inv)

        x = bdmm(a_ak, v)
        u_loc = bdmm(t_inv, x)
        ta = bdmm(t_inv, at)
        gb = fold(_dot_tn(bh, ta))
        hm = fold(_dot_tn(jnp.concatenate([bh, kh], axis=0), jnp.concatenate([u_loc, v], axis=0)))
        q = rt + bdmm(a_rb, ta)
        y_loc = bdmm(a_rb, u_loc) + bdmm(a_rk, v)

        m0 = state_ref[g]
        ys.append(bdmm(q, m0) + y_loc)
        e_diag = jnp.where(eye, jnp.broadcast_to(jnp.exp(tot), (L, GW)), 0.0)
        w_sbs = _dot_f32(e_diag, seg_ones.astype(F32))
        state_ref[g] = w_sbs * m0 + bdmm(gb, m0) + hm

    y_ref[0, 0] = jnp.concatenate(ys, axis=1) if n_groups > 1 else ys[0]
    bonus_ref[0, 0] = jnp.concatenate(bonuses, axis=1) if n_groups > 1 else bonuses[0]


def wkv_bidir(r, k, v, wz, az, k_k, k_a, r_k, *, n_groups=2):
    bsz, seq, dm = r.shape
    assert CHUNK == HEAD and seq % CHUNK == 0
    n_groups = min(n_groups, dm // GW)
    w = n_groups * GW
    assert dm % w == 0
    nc = seq // CHUNK

    def cidx(d, c):
        return c + d * (nc - 1 - 2 * c)

    x_spec = pl.BlockSpec((1, CHUNK, w), lambda d, b, j, c: (b, cidx(d, c), j))
    d_spec = pl.BlockSpec((1, 1, CHUNK, w), lambda d, b, j, c: (d, b, cidx(d, c), j))
    p_spec = pl.BlockSpec((1, w), lambda d, b, j, c: (0, j))
    out_sd = jax.ShapeDtypeStruct((2, bsz, seq, dm), F32)
    return pl.pallas_call(
        functools.partial(_wkv_kernel, n_groups=n_groups),
        grid=(2, bsz, dm // w, nc),
        in_specs=[x_spec, x_spec, x_spec, d_spec, d_spec, p_spec, p_spec, p_spec],
        out_specs=[d_spec, d_spec],
        out_shape=[out_sd, out_sd],
        scratch_shapes=[pltpu.VMEM((n_groups, HEAD, GW), F32)],
        compiler_params=_cparams(("parallel", "parallel", "parallel", "arbitrary")),
        name="wkv_bidir",
    )(r, k, v, wz, az, k_k.reshape(1, dm), k_a.reshape(1, dm), r_k.reshape(1, dm))


def _rms(x, g):
    return x * lax.rsqrt(jnp.mean(x * x, axis=-1, keepdims=True) + RMS_EPS) * g


def _sigmoid(x):
    return 1.0 / (1.0 + jnp.exp(-x))


def _rmsnorm_kernel(x_ref, g_ref, o_ref):
    o_ref[...] = _rms(x_ref[...], g_ref[...]).astype(o_ref.dtype)


def rmsnorm(x, g, *, out_dtype, tm=512):
    m, dm = x.shape
    tm = min(tm, m)
    return pl.pallas_call(
        _rmsnorm_kernel,
        grid=(m // tm,),
        in_specs=[pl.BlockSpec((tm, dm), lambda i: (i, 0)), pl.BlockSpec((1, dm), lambda i: (0, 0))],
        out_specs=pl.BlockSpec((tm, dm), lambda i: (i, 0)),
        out_shape=jax.ShapeDtypeStruct((m, dm), out_dtype),
        compiler_params=_cparams(("parallel",)),
        name="rmsnorm",
    )(x, g.reshape(1, dm))


def _rwkv_pre_kernel(x_ref, xp_ref, xn_ref, g_ref, mu_ref, *o_refs, ts):
    i = pl.program_id(1)
    g = g_ref[...]
    hn = _rms(x_ref[0], g)
    h_before = jnp.where(i == 0, 0.0, _rms(xp_ref[0], g)[7:8])
    h_after = jnp.where(i == pl.num_programs(1) - 1, 0.0, _rms(xn_ref[0], g)[0:1])
    row = lax.broadcasted_iota(jnp.int32, hn.shape, 0)
    prev = jnp.where(row == 0, h_before, pltpu.roll(hn, 1, 0))
    nxt = jnp.where(row == ts - 1, h_after, pltpu.roll(hn, ts - 1, 0))
    delta = 0.5 * (prev + nxt) - hn
    for n, o_ref in enumerate(o_refs):
        o_ref[0] = (hn + delta * mu_ref[n:n + 1, :]).astype(o_ref.dtype)


def rwkv_pre(h, g, mu, *, ts=256):
    bsz, seq, dm = h.shape
    ts = min(ts, seq)
    n_mix = mu.shape[0]
    sub = 8
    nsub = ts // sub
    x_spec = pl.BlockSpec((1, ts, dm), lambda b, i: (b, i, 0))
    return pl.pallas_call(
        functools.partial(_rwkv_pre_kernel, ts=ts),
        grid=(bsz, seq // ts),
        in_specs=[x_spec,
                  pl.BlockSpec((1, sub, dm), lambda b, i: (b, jnp.maximum(i * nsub - 1, 0), 0)),
                  pl.BlockSpec((1, sub, dm), lambda b, i: (b, jnp.minimum((i + 1) * nsub, seq // sub - 1), 0)),
                  pl.BlockSpec((1, dm), lambda b, i: (0, 0)),
                  pl.BlockSpec((n_mix, dm), lambda b, i: (0, 0))],
        out_specs=[x_spec] * n_mix,
        out_shape=[jax.ShapeDtypeStruct((bsz, seq, dm), BF16)] * n_mix,
        compiler_params=_cparams(("parallel", "parallel")),
        name="rwkv_pre",
    )(h, h, h, g.reshape(1, dm), mu)


def _lora_kernel(x_ref, w1_ref, w2_ref, b_ref, o_ref, *, act):
    hid = _dot(x_ref[...], w1_ref[0])
    if act == "tanh":
        hid = jnp.tanh(hid)
    elif act == "sigmoid":
        hid = _sigmoid(hid)
    o_ref[0] = (_dot(hid, w2_ref[0]) + b_ref[0]).astype(o_ref.dtype)


def lora(x, w1, w2, bias, *, act, tm=512):
    m, dm = x.shape
    nd, _, rank = w1.shape
    tm = min(tm, m)
    rpad = -rank % LANES
    w1 = jnp.pad(w1, ((0, 0), (0, 0), (0, rpad))).astype(BF16)
    w2 = jnp.pad(w2, ((0, 0), (0, rpad), (0, 0))).astype(BF16)
    rp = rank + rpad
    return pl.pallas_call(
        functools.partial(_lora_kernel, act=act),
        grid=(nd, m // tm),
        in_specs=[pl.BlockSpec((tm, dm), lambda d, i: (i, 0)),
                  pl.BlockSpec((1, dm, rp), lambda d, i: (d, 0, 0)),
                  pl.BlockSpec((1, rp, dm), lambda d, i: (d, 0, 0)),
                  pl.BlockSpec((1, 1, dm), lambda d, i: (d, 0, 0))],
        out_specs=pl.BlockSpec((1, tm, dm), lambda d, i: (d, i, 0)),
        out_shape=jax.ShapeDtypeStruct((nd, m, dm), F32),
        compiler_params=_cparams(("parallel", "parallel")),
        name="lora_" + act,
    )(x, w1, w2, bias.reshape(nd, 1, dm))


def _rwkv_post_kernel(y_ref, b_ref, g_ref, lw_ref, lb_ref, o_ref):
    dm = o_ref.shape[-1]
    seg = (lax.broadcasted_iota(jnp.int32, (GW, GW), 0) // HEAD
           == lax.broadcasted_iota(jnp.int32, (GW, GW), 1) // HEAD).astype(BF16)
    for s in range(dm // GW):
        sl = slice(s * GW, (s + 1) * GW)
        y = y_ref[0, :, sl] + y_ref[1, :, sl]
        mean = _dot(y, seg) * (1.0 / HEAD)
        yc = y - mean
        var = _dot(yc * yc, seg) * (1.0 / HEAD)
        yn = yc * lax.rsqrt(var + GN_EPS) * lw_ref[:, sl] + lb_ref[:, sl]
        out = yn + (b_ref[0, :, sl] + b_ref[1, :, sl])
        o_ref[:, sl] = (out * g_ref[0, :, sl]).astype(o_ref.dtype)


def rwkv_post(y, bonus, g, ln_w, ln_b, *, tm=256):
    _, m, dm = y.shape
    tm = min(tm, m)
    yspec = pl.BlockSpec((2, tm, dm), lambda i: (0, i, 0))
    pspec = pl.BlockSpec((1, dm), lambda i: (0, 0))
    return pl.pallas_call(
        _rwkv_post_kernel,
        grid=(m // tm,),
        in_specs=[yspec, yspec, pl.BlockSpec((1, tm, dm), lambda i: (0, i, 0)), pspec, pspec],
        out_specs=pl.BlockSpec((tm, dm), lambda i: (i, 0)),
        out_shape=jax.ShapeDtypeStruct((m, dm), BF16),
        compiler_params=_cparams(("parallel",)),
        name="rwkv_post",
    )(y, bonus, g, ln_w.reshape(1, dm), ln_b.reshape(1, dm))


def _ffn_kernel(be_ref, x_ref, wg_ref, wu_ref, wd_ref, *rest, has_res):
    o_ref = rest[-1]
    f = pl.program_id(1)

    @pl.when(f == 0)
    def _():
        o_ref[...] = rest[0][...] if has_res else jnp.zeros_like(o_ref)

    x = x_ref[...].astype(BF16)
    gate = _dot(x, wg_ref[0])
    up = _dot(x, wu_ref[0])
    o_ref[...] += _dot(gate * _sigmoid(gate) * up, wd_ref[0])


def ffn(x, blk_e, wg, wu, wd, *, res=None, tm, tf):
    p, dm = x.shape
    fdim = wg.shape[-1]
    assert p % tm == 0 and fdim % tf == 0
    row_spec = pl.BlockSpec((tm, dm), lambda i, f, be: (i, 0))
    in_specs = [row_spec,
                pl.BlockSpec((1, dm, tf), lambda i, f, be: (be[i], 0, f)),
                pl.BlockSpec((1, dm, tf), lambda i, f, be: (be[i], 0, f)),
                pl.BlockSpec((1, tf, dm), lambda i, f, be: (be[i], f, 0))]
    args = [x, wg, wu, wd]
    if res is not None:
        in_specs.append(row_spec)
        args.append(res)
    return pl.pallas_call(
        functools.partial(_ffn_kernel, has_res=res is not None),
        grid_spec=pltpu.PrefetchScalarGridSpec(
            num_scalar_prefetch=1, grid=(p // tm, fdim // tf),
            in_specs=in_specs, out_specs=row_spec),
        out_shape=jax.ShapeDtypeStruct((p, dm), F32),
        compiler_params=_cparams(("parallel", "arbitrary")),
        name="ffn",
    )(blk_e, *args)


def _rope_kernel(q_ref, k_ref, v_ref, pos_ref, inv_ref, qo_ref, ko_ref, vo_ref):
    tm, dm = qo_ref.shape
    ang = pos_ref[...].astype(F32) * inv_ref[...]
    cos = jnp.cos(ang)
    sin = jnp.sin(ang)
    lane = lax.broadcasted_iota(jnp.int32, (tm, LANES), 1)
    first = (lane % HEAD) < HEAD // 2
    sin_signed = jnp.where(first, -sin, sin)
    for s in range(dm // LANES):
        sl = slice(s * LANES, (s + 1) * LANES)
        for src, dst, scale in ((q_ref, qo_ref, HEAD ** -0.5), (k_ref, ko_ref, 1.0)):
            t = src[:, sl]
            partner = jnp.where(first, pltpu.roll(t, LANES - HEAD // 2, 1), pltpu.roll(t, HEAD // 2, 1))
            dst[:, sl] = ((t * cos + partner * sin_signed) * scale).astype(dst.dtype)
    vo_ref[...] = v_ref[...].astype(vo_ref.dtype)


def rope_qkv(qkv, positions, *, tm=256):
    m, dm3 = qkv.shape
    dm = dm3 // 3
    tm = min(tm, m)
    half = HEAD // 2
    inv = ROPE_THETA ** (-(2.0 * (jnp.arange(LANES) % half)).astype(F32) / HEAD)
    specs = [pl.BlockSpec((tm, dm), lambda i, c=c: (i, c)) for c in range(3)]
    o_spec = pl.BlockSpec((tm, dm), lambda i: (i, 0))
    return pl.pallas_call(
        _rope_kernel,
        grid=(m // tm,),
        in_specs=specs + [pl.BlockSpec((tm, 1), lambda i: (i, 0)), pl.BlockSpec((1, LANES), lambda i: (0, 0))],
        out_specs=[o_spec] * 3,
        out_shape=[jax.ShapeDtypeStruct((m, dm), BF16)] * 3,
        compiler_params=_cparams(("parallel",)),
        name="rope_qkv",
    )(qkv, qkv, qkv, positions.reshape(m, 1), inv.reshape(1, LANES))


def _attn_kernel(q_ref, k_ref, v_ref, lq1_ref, lk1_ref, lq2_ref, lk2_ref, sub_ref, o_ref,
                 m_ref, l_ref, acc_ref, *, lambda_init):
    kv = pl.program_id(3)

    @pl.when(kv == 0)
    def _():
        m_ref[...] = jnp.full_like(m_ref, -jnp.inf)
        l_ref[...] = jnp.zeros_like(l_ref)
        acc_ref[...] = jnp.zeros_like(acc_ref)

    q = q_ref[...]
    k = k_ref[...]
    v = v_ref[...]
    lane = lax.broadcasted_iota(jnp.int32, q.shape, 1)
    for c in range(2):
        qc = jnp.where((lane >= c * HEAD) & (lane < (c + 1) * HEAD), q, jnp.zeros((), q.dtype))
        s = _dot_nt(qc, k)
        m_prev = m_ref[c]
        m_new = jnp.maximum(m_prev, jnp.max(s, axis=1, keepdims=True))
        alpha = jnp.exp(m_prev - m_new)
        p = jnp.exp(s - m_new)
        l_ref[c] = alpha * l_ref[c] + jnp.sum(p, axis=1, keepdims=True)
        acc_ref[c] = alpha * acc_ref[c] + _dot(p, v)
        m_ref[c] = m_new

    @pl.when(kv == pl.num_programs(3) - 1)
    def _():
        lam = (jnp.exp(jnp.sum(lq1_ref[...] * lk1_ref[...], keepdims=True))
               - jnp.exp(jnp.sum(lq2_ref[...] * lk2_ref[...], keepdims=True)) + lambda_init)
        o = acc_ref[0] / l_ref[0] - lam * (acc_ref[1] / l_ref[1])
        o = o * lax.rsqrt(jnp.mean(o * o, axis=-1, keepdims=True) + SUBLN_EPS) * sub_ref[...] * (1.0 - lambda_init)
        o_ref[...] = o.astype(o_ref.dtype)


def diff_attn(q, k, v, lq1, lk1, lq2, lk2, subln, lambda_init, *, bsz, tq=512, tk=512):
    m, dm = q.shape
    seq = m // bsz
    hw = 2 * HEAD
    tq, tk = min(tq, seq), min(tk, seq)
    nq, nk = seq // tq, seq // tk
    q_spec = pl.BlockSpec((tq, hw), lambda b, h, i, j: (b * nq + i, h))
    kv_spec = pl.BlockSpec((tk, hw), lambda b, h, i, j: (b * nk + j, h))
    l_spec = pl.BlockSpec((1, HEAD), lambda b, h, i, j: (0, 0))
    return pl.pallas_call(
        functools.partial(_attn_kernel, lambda_init=lambda_init),
        grid=(bsz, dm // hw, nq, nk),
        in_specs=[q_spec, kv_spec, kv_spec, l_spec, l_spec, l_spec, l_spec,
                  pl.BlockSpec((1, hw), lambda b, h, i, j: (0, 0))],
        out_specs=q_spec,
        out_shape=jax.ShapeDtypeStruct((m, dm), BF16),
        scratch_shapes=[pltpu.VMEM((2, tq, 1), F32), pltpu.VMEM((2, tq, 1), F32), pltpu.VMEM((2, tq, hw), F32)],
        compiler_params=_cparams(("parallel", "parallel", "parallel", "arbitrary")),
        name="diff_attn",
    )(q, k, v, lq1.reshape(1, HEAD), lk1.reshape(1, HEAD), lq2.reshape(1, HEAD), lk2.reshape(1, HEAD),
      subln.reshape(1, hw))


def _router_kernel(h_ref, g_ref, wr_ref, hn_ref, e_ref, gate_ref, rank_ref, cnt_ref, *, n_exp):
    tm = h_ref.shape[0]

    @pl.when(pl.program_id(0) == 0)
    def _():
        cnt_ref[...] = jnp.zeros_like(cnt_ref)

    hn = _rms(h_ref[...], g_ref[...])
    hn_ref[...] = hn
    lane = lax.broadcasted_iota(jnp.int32, (tm, LANES), 1)
    logits = jnp.where(lane < n_exp, _dot_f32(hn, wr_ref[...]), -jnp.inf)
    m1 = jnp.max(logits, axis=1, keepdims=True)
    i1 = jnp.min(jnp.where(logits == m1, lane, LANES), axis=1, keepdims=True)
    rest = jnp.where(lane == i1, -jnp.inf, logits)
    m2 = jnp.max(rest, axis=1, keepdims=True)
    i2 = jnp.min(jnp.where(rest == m2, lane, LANES), axis=1, keepdims=True)
    e21 = jnp.exp(m2 - m1)
    g1 = 1.0 / (1.0 + e21)
    g2 = e21 / (1.0 + e21)
    oh1 = lane == i1
    oh2 = lane == i2
    both = jnp.where(oh1 | oh2, 1.0, 0.0)
    earlier = (lax.broadcasted_iota(jnp.int32, (tm, tm), 0) > lax.broadcasted_iota(jnp.int32, (tm, tm), 1))
    cum = _dot(earlier.astype(BF16), both) + cnt_ref[...]
    r1 = jnp.sum(jnp.where(oh1, cum, 0.0), axis=1, keepdims=True)
    r2 = jnp.sum(jnp.where(oh2, cum, 0.0), axis=1, keepdims=True)
    cnt_ref[...] += jnp.sum(both, axis=0, keepdims=True)
    e_ref[...] = jnp.where(lane == 0, i1, jnp.where(lane == 1, i2, 0))
    gate_ref[...] = jnp.where(lane == 0, g1, jnp.where(lane == 1, g2, 0.0))
    rank_ref[...] = jnp.where(lane == 0, r1, jnp.where(lane == 1, r2, 0.0)).astype(jnp.int32)


def moe_route(h, g, w_router, *, tm=512):
    m, dm = h.shape
    n_exp = w_router.shape[1]
    tm = min(tm, m)
    wr = jnp.pad(w_router, ((0, 0), (0, LANES - n_exp)))
    row = lambda dt: jax.ShapeDtypeStruct((m, LANES), dt)
    lspec = pl.BlockSpec((tm, LANES), lambda i: (i, 0))
    return pl.pallas_call(
        functools.partial(_router_kernel, n_exp=n_exp),
        grid=(m // tm,),
        in_specs=[pl.BlockSpec((tm, dm), lambda i: (i, 0)), pl.BlockSpec((1, dm), lambda i: (0, 0)),
                  pl.BlockSpec((dm, LANES), lambda i: (0, 0))],
        out_specs=[pl.BlockSpec((tm, dm), lambda i: (i, 0)), lspec, lspec, lspec,
                   pl.BlockSpec((1, LANES), lambda i: (0, 0))],
        out_shape=[jax.ShapeDtypeStruct((m, dm), F32), row(jnp.int32), row(F32), row(jnp.int32),
                   jax.ShapeDtypeStruct((1, LANES), F32)],
        compiler_params=_cparams(("arbitrary",)),
        name="moe_route",
    )(h, g.reshape(1, dm), wr)


def _dest_kernel(e_ref, rank_ref, cnt_ref, dest_ref, be_ref, *, n_exp, bm):
    tm = e_ref.shape[0]
    nb = be_ref.shape[0]
    cnt = cnt_ref[...]
    pcnt = jnp.floor((cnt + (bm - 1)) * (1.0 / bm)) * bm
    upper = (lax.broadcasted_iota(jnp.int32, (LANES, LANES), 0)
             <= lax.broadcasted_iota(jnp.int32, (LANES, LANES), 1)).astype(F32)
    pends = _dot_f32(jnp.broadcast_to(pcnt, (8, LANES)), upper)[0:1]
    pstart = pends - pcnt
    lane = lax.broadcasted_iota(jnp.int32, (tm, LANES), 1)
    e = e_ref[...]
    d1 = jnp.sum(jnp.where(lane == e[:, 0:1], pstart, 0.0), axis=1, keepdims=True)
    d2 = jnp.sum(jnp.where(lane == e[:, 1:2], pstart, 0.0), axis=1, keepdims=True)
    dest = jnp.where(lane == 0, d1, jnp.where(lane == 1, d2, 0.0)).astype(jnp.int32) + rank_ref[...]
    dest_ref[...] = dest
    blk_lane = lax.broadcasted_iota(jnp.int32, (nb, LANES), 1)
    blk_row = lax.broadcasted_iota(jnp.int32, (nb, LANES), 0).astype(F32) * bm
    ended = jnp.where((blk_lane < n_exp) & (pends <= blk_row), 1.0, 0.0)
    be = jnp.minimum(jnp.sum(ended, axis=1, keepdims=True), n_exp - 1.0)
    be_ref[...] = jnp.broadcast_to(be, (nb, LANES)).astype(jnp.int32)


def moe_dest(e, rank, cnt, *, n_exp, bm, nblk, tm=512):
    m = e.shape[0]
    tm = min(tm, m)
    nb = -(-nblk // 8) * 8
    lspec = pl.BlockSpec((tm, LANES), lambda i: (i, 0))
    return pl.pallas_call(
        functools.partial(_dest_kernel, n_exp=n_exp, bm=bm),
        grid=(m // tm,),
        in_specs=[lspec, lspec, pl.BlockSpec((1, LANES), lambda i: (0, 0))],
        out_specs=[lspec, pl.BlockSpec((nb, LANES), lambda i: (0, 0))],
        out_shape=[jax.ShapeDtypeStruct((m, LANES), jnp.int32), jax.ShapeDtypeStruct((nb, LANES), jnp.int32)],
        compiler_params=_cparams(("arbitrary",)),
        name="moe_dest",
    )(e, rank, cnt)


def _row_copy(src, dst, sem):
    return pltpu.make_async_copy(src, dst, sem)


def _dispatch_kernel(dest_ref, hn_ref, xs_in_ref, xs_ref, sem):
    del xs_in_ref
    tm = hn_ref.shape[0]
    base = pl.program_id(0) * tm

    def issue(r, carry):
        for s in range(TOP_K):
            dst = dest_ref[TOP_K * (base + r) + s]
            _row_copy(hn_ref.at[pl.ds(r, 1)], xs_ref.at[pl.ds(dst, 1)], sem).start()
        return carry

    lax.fori_loop(0, tm, issue, 0)

    def drain(r, carry):
        for s in range(TOP_K):
            _row_copy(hn_ref.at[pl.ds(0, 1)], xs_ref.at[pl.ds(0, 1)], sem).wait()
        return carry

    lax.fori_loop(0, tm, drain, 0)


def moe_dispatch(dest_flat, hn, n_rows, *, tm=256):
    m, dm = hn.shape
    tm = min(tm, m)
    xs0 = jnp.zeros((n_rows, dm), hn.dtype)
    return pl.pallas_call(
        _dispatch_kernel,
        grid_spec=pltpu.PrefetchScalarGridSpec(
            num_scalar_prefetch=1, grid=(m // tm,),
            in_specs=[pl.BlockSpec((tm, dm), lambda i, d: (i, 0)), pl.BlockSpec(memory_space=pl.ANY)],
            out_specs=pl.BlockSpec(memory_space=pl.ANY),
            scratch_shapes=[pltpu.SemaphoreType.DMA(())]),
        out_shape=jax.ShapeDtypeStruct((n_rows, dm), hn.dtype),
        input_output_aliases={2: 0},
        compiler_params=_cparams(("arbitrary",)),
        name="moe_dispatch",
    )(dest_flat, hn, xs0)


def _combine_kernel(dest_ref, yb_ref, gate_ref, res_ref, o_ref, buf_ref, sem):
    tm = res_ref.shape[0]
    base = pl.program_id(0) * tm

    def issue(r, carry):
        for s in range(TOP_K):
            src = dest_ref[TOP_K * (base + r) + s]
            _row_copy(yb_ref.at[pl.ds(src, 1)], buf_ref.at[s, pl.ds(r, 1)], sem).start()
        return carry

    lax.fori_loop(0, tm, issue, 0)

    def drain(r, carry):
        for s in range(TOP_K):
            _row_copy(yb_ref.at[pl.ds(0, 1)], buf_ref.at[s, pl.ds(0, 1)], sem).wait()
        return carry

    lax.fori_loop(0, tm, drain, 0)
    gate = gate_ref[...]
    o_ref[...] = res_ref[...] + (gate[:, 0:1] * buf_ref[0] + gate[:, 1:2] * buf_ref[1])


def moe_combine(dest_flat, yb, gates, res, *, tm=256):
    m, dm = res.shape
    tm = min(tm, m)
    row_spec = pl.BlockSpec((tm, dm), lambda i, d: (i, 0))
    return pl.pallas_call(
        _combine_kernel,
        grid_spec=pltpu.PrefetchScalarGridSpec(
            num_scalar_prefetch=1, grid=(m // tm,),
            in_specs=[pl.BlockSpec(memory_space=pl.ANY), pl.BlockSpec((tm, LANES), lambda i, d: (i, 0)), row_spec],
            out_specs=row_spec,
            scratch_shapes=[pltpu.VMEM((TOP_K, tm, dm), F32), pltpu.SemaphoreType.DMA(())]),
        out_shape=jax.ShapeDtypeStruct((m, dm), F32),
        compiler_params=_cparams(("arbitrary",)),
        name="moe_combine",
    )(dest_flat, yb, gates, res)


def moe_layer(h, g, w_router, wg, wu, wd):
    m, dm = h.shape
    n_exp = w_router.shape[1]
    bm = 1024 if m * TOP_K >= 8 * 1024 else 128
    n_rows = m * TOP_K + n_exp * bm
    nblk = n_rows // bm
    hn, e, gates, rank, cnt = moe_route(h, g, w_router)
    dest, be = moe_dest(e, rank, cnt, n_exp=n_exp, bm=bm, nblk=nblk)
    dest_flat = dest[:, :TOP_K].reshape(-1)
    xs = moe_dispatch(dest_flat, hn, n_rows)
    tf = 256 if wg.shape[-1] % 256 == 0 else LANES
    yb = ffn(xs, be[:nblk, 0], wg, wu, wd, tm=bm, tf=tf)
    return moe_combine(dest_flat, yb, gates, h)


def rwkv_layer(h, bsz, g_norm, mu, w_rkv, g1, g2, w0, w1, w2, a0, a1, a2, k_k, k_a, r_k, ln_w, ln_b, w_o):
    m, dm = h.shape
    seq = m // bsz
    xr, xk, xv, xw, xa, xg = (t.reshape(m, dm) for t in rwkv_pre(h.reshape(bsz, seq, dm), g_norm, mu))
    w_rkv = w_rkv.astype(BF16)
    r, k, v = (matmul(xi, w_rkv, n=dm, col_off=n * dm).reshape(bsz, seq, dm)
               for n, xi in enumerate((xr, xk, xv)))
    gate = lora(xg, g1[None], g2[None], jnp.zeros((1, dm), F32), act="sigmoid")
    wz = lora(xw, w1, w2, w0, act="tanh").reshape(2, bsz, seq, dm)
    az = lora(xa, a1, a2, a0, act="none").reshape(2, bsz, seq, dm)
    y, bonus = wkv_bidir(r, k, v, wz, az, k_k, k_a, r_k)
    z = rwkv_post(y.reshape(2, m, dm), bonus.reshape(2, m, dm), gate, ln_w, ln_b)
    return matmul(z, w_o.astype(BF16), res=h)


def attn_layer(h, bsz, positions, g_norm, w_qkv, lq1, lk1, lq2, lk2, subln, w_o, lambda_init):
    hn = rmsnorm(h, g_norm, out_dtype=BF16)
    qkv = matmul(hn, w_qkv.astype(BF16))
    q, k, v = rope_qkv(qkv, positions)
    o = diff_attn(q, k, v, lq1, lk1, lq2, lk2, subln, lambda_init, bsz=bsz)
    return matmul(o, w_o.astype(BF16), res=h)


def dense_ffn_layer(h, g_norm, wg, wu, wd):
    m = h.shape[0]
    hn = rmsnorm(h, g_norm, out_dtype=BF16)
    tm = min(512, m)
    tf = 512 if wg.shape[-1] % 512 == 0 else LANES
    return ffn(hn, jnp.zeros((m // tm,), jnp.int32), wg.astype(BF16)[None], wu.astype(BF16)[None],
               wd.astype(BF16)[None], res=h, tm=tm, tf=tf)


def kernel(x, positions, norm_mix, norm_ffn, norm_final, rw_mu, rw_w_rkv, rw_g1, rw_g2, rw_w0, rw_w1, rw_w2, rw_a0, rw_a1, rw_a2, rw_kk, rw_ka, rw_rk, rw_ln_w, rw_ln_b, rw_w_o, da_w_qkv, da_lq1, da_lk1, da_lq2, da_lk2, da_subln, da_w_o, ff_wg, ff_wu, ff_wd, moe_router, moe_wg, moe_wu, moe_wd):
    bsz, seq, dm = x.shape
    depth = norm_mix.shape[0]
    h = x.reshape(bsz * seq, dm)
    for i in range(depth):
        j = i // 2
        if i % 2 == 0:
            h = rwkv_layer(h, bsz, norm_mix[i], rw_mu[j], rw_w_rkv[j], rw_g1[j], rw_g2[j], rw_w0[j], rw_w1[j],
                           rw_w2[j], rw_a0[j], rw_a1[j], rw_a2[j], rw_kk[j], rw_ka[j], rw_rk[j],
                           rw_ln_w[j], rw_ln_b[j], rw_w_o[j])
            h = dense_ffn_layer(h, norm_ffn[i], ff_wg[j], ff_wu[j], ff_wd[j])
        else:
            lambda_init = 0.8 - 0.6 * math.exp(-0.3 * i)
            h = attn_layer(h, bsz, positions, norm_mix[i], da_w_qkv[j], da_lq1[j], da_lk1[j], da_lq2[j],
                           da_lk2[j], da_subln[j], da_w_o[j], lambda_init)
            h = moe_layer(h, norm_ffn[i], moe_router[j], moe_wg[j], moe_wu[j], moe_wd[j])
    return rmsnorm(h, norm_final, out_dtype=x.dtype).reshape(bsz, seq, dm)
```

```python
import functools
import math

import jax
import jax.numpy as jnp
from jax import lax
from jax.experimental import pallas as pl
from jax.experimental.pallas import tpu as pltpu

F32 = jnp.float32
BF16 = jnp.bfloat16

RMS_EPS = 1e-6
HEAD = 64
GN_EPS = HEAD * 1e-5
ROPE_THETA = 10000.0
SUBLN_EPS = 1e-5
LOG2E = math.log2(math.e)
TOP_K = 2
LANES = 128
CHUNK = 64
GROUP = 4
GW = GROUP * HEAD
VMEM_LIMIT = 56 * 1024 * 1024


def _cparams(sem, vmem=VMEM_LIMIT):
    return pltpu.CompilerParams(dimension_semantics=sem, vmem_limit_bytes=vmem)


def _dot(a, b):
    return jnp.dot(a.astype(BF16), b.astype(BF16), preferred_element_type=F32)


def _dot_f32(a, b):
    return jnp.dot(a, b, preferred_element_type=F32, precision=lax.Precision.HIGHEST)


def _split2(x):
    hi = x.astype(BF16)
    return hi, (x - hi.astype(F32)).astype(BF16)


def _dot_nt(a, b):
    return lax.dot_general(a.astype(BF16), b.astype(BF16), (((1,), (1,)), ((), ())),
                           preferred_element_type=F32)


def _dot_tn(a, b):
    return lax.dot_general(a.astype(BF16), b.astype(BF16), (((0,), (0,)), ((), ())),
                           preferred_element_type=F32)


def _mm_kernel(*refs, has_bias, has_res):
    a_ref, b_ref = refs[0], refs[1]
    o_ref = refs[-1]
    acc = _dot(a_ref[...], b_ref[...])
    pos = 2
    if has_bias:
        acc = acc + refs[pos][...]
        pos += 1
    if has_res:
        acc = acc + refs[pos][...]
    o_ref[...] = acc.astype(o_ref.dtype)


def matmul(a, b, *, n=None, col_off=0, bias=None, res=None, out_dtype=F32, tm=1024, tn=512):
    m, k = a.shape
    n = b.shape[1] if n is None else n
    tm, tn = min(tm, m), min(tn, n)
    assert m % tm == 0 and n % tn == 0 and col_off % tn == 0
    off = col_off // tn
    in_specs = [pl.BlockSpec((tm, k), lambda i, j: (i, 0)),
                pl.BlockSpec((k, tn), lambda i, j: (0, j + off))]
    args = [a, b]
    if bias is not None:
        in_specs.append(pl.BlockSpec((1, tn), lambda i, j: (0, j)))
        args.append(bias.reshape(1, n))
    if res is not None:
        in_specs.append(pl.BlockSpec((tm, tn), lambda i, j: (i, j)))
        args.append(res)
    return pl.pallas_call(
        functools.partial(_mm_kernel, has_bias=bias is not None, has_res=res is not None),
        grid=(m // tm, n // tn),
        in_specs=in_specs,
        out_specs=pl.BlockSpec((tm, tn), lambda i, j: (i, j)),
        out_shape=jax.ShapeDtypeStruct((m, n), out_dtype),
        compiler_params=_cparams(("parallel", "parallel")),
        name="matmul",
    )(*args)


def _wkv_kernel(r_ref, k_ref, v_ref, wz_ref, az_ref, kk_ref, ka_ref, rk_ref,
                y_ref, bonus_ref, state_ref, *, n_groups):
    L = CHUNK
    d = pl.program_id(0)
    sgn = 1 - 2 * d

    @pl.when(pl.program_id(3) == 0)
    def _():
        state_ref[...] = jnp.zeros_like(state_ref)

    row = lax.broadcasted_iota(jnp.int32, (L, GW), 0)
    col = lax.broadcasted_iota(jnp.int32, (L, GW), 1) % L
    diff = (row - col) * sgn
    strict = diff > 0
    incl = diff >= 0
    eye = diff == 0
    level_masks = []
    s = 1
    while s < L:
        level_masks.append(strict & ((row // (2 * s)) == (col // (2 * s))) & ((row // s) != (col // s)))
        s *= 2
    bd_mask = (lax.broadcasted_iota(jnp.int32, (GROUP * L, GW), 0) // L
               == lax.broadcasted_iota(jnp.int32, (GROUP * L, GW), 1) // HEAD)
    seg_ones = bd_mask.astype(BF16)
    tri = (lax.broadcasted_iota(jnp.int32, (L, L), 0) - lax.broadcasted_iota(jnp.int32, (L, L), 1)) * sgn >= 0
    tri = tri.astype(F32)

    def bd(x):
        xb = x.astype(BF16)
        return jnp.where(bd_mask, jnp.concatenate([xb] * GROUP, axis=0), jnp.zeros((), BF16))

    def bdmm(x, y):
        return jnp.dot(x.astype(BF16), bd(y), preferred_element_type=F32)

    def fold(full):
        fm = jnp.where(bd_mask, full, 0.0)
        out = fm[0:HEAD]
        for j in range(1, GROUP):
            out = out + fm[j * HEAD:(j + 1) * HEAD]
        return out

    groups = range(n_groups)
    sls = [slice(g * GW, (g + 1) * GW) for g in groups]
    r = [r_ref[0, :, sl] for sl in sls]
    k = [k_ref[0, :, sl] for sl in sls]
    v = [v_ref[0, :, sl] for sl in sls]

    lw, a_s = [], []
    for sl in sls:
        u = -wz_ref[0, 0, :, sl]
        softplus = jnp.maximum(u, 0.0) + jnp.log(1.0 + jnp.exp(-jnp.abs(u)))
        lw.append(-jnp.exp(-softplus - 0.5))
        a_s.append(1.0 / (1.0 + jnp.exp(-az_ref[0, 0, :, sl])))
    split = [_split2(x) for x in lw]
    c = [_dot(tri, hi) + _dot(tri, lo) for hi, lo in split]
    kkr = [k[g] * kk_ref[:, sls[g]] for g in groups]
    ss = [_dot(x * x, seg_ones) for x in kkr]
    kk = [kkr[g] / jnp.maximum(jnp.sqrt(ss[g]), 1e-12) for g in groups]
    kd = [k[g] * (1.0 + (a_s[g] - 1.0) * ka_ref[:, sls[g]]) for g in groups]
    b_vec = [kk[g] * a_s[g] for g in groups]
    rkd = [_dot(r[g] * kd[g] * rk_ref[:, sls[g]], seg_ones) for g in groups]
    bonus_ref[0, 0] = jnp.concatenate([rkd[g] * v[g] for g in groups], axis=1)

    tot = [jnp.sum(x, axis=0, keepdims=True) for x in lw]
    e_nc = [jnp.exp(-x) for x in c]
    e_tc = [jnp.exp(tot[g] - c[g]) for g in groups]
    at = [-kk[g] * jnp.exp(c[g] - lw[g]) for g in groups]
    rt = [r[g] * jnp.exp(c[g]) for g in groups]
    bt = [b_vec[g] * e_nc[g] for g in groups]
    kt = [kd[g] * e_nc[g] for g in groups]
    bh = [b_vec[g] * e_tc[g] for g in groups]
    kh = [kd[g] * e_tc[g] for g in groups]

    ar = [jnp.concatenate([at[g], rt[g]], axis=0) for g in groups]
    pb = [_dot_nt(ar[g], bd(bt[g])) for g in groups]
    pk = [_dot_nt(ar[g], bd(kt[g])) for g in groups]
    a_ab = [jnp.where(strict, x[:L], 0.0) for x in pb]
    a_rb = [jnp.where(incl, x[L:], 0.0) for x in pb]
    a_ak = [jnp.where(strict, x[:L], 0.0) for x in pk]
    a_rk = [jnp.where(incl, x[L:], 0.0) for x in pk]
    x_loc = [bdmm(a_ak[g], v[g]) for g in groups]

    t_inv = [jnp.where(eye, 1.0, 0.0) + jnp.where(level_masks[0], a, 0.0) for a in a_ab]
    for m in level_masks[1:]:
        t1 = [bdmm(t_inv[g], jnp.where(m, a_ab[g], 0.0)) for g in groups]
        t_inv = [t_inv[g] + bdmm(t1[g], t_inv[g]) for g in groups]

    u_loc = [bdmm(t_inv[g], x_loc[g]) for g in groups]
    ta = [bdmm(t_inv[g], at[g]) for g in groups]
    gb = [fold(_dot_tn(bh[g], ta[g])) for g in groups]
    hm = [fold(_dot_tn(jnp.concatenate([bh[g], kh[g]], axis=0), jnp.concatenate([u_loc[g], v[g]], axis=0)))
          for g in groups]
    q = [rt[g] + bdmm(a_rb[g], ta[g]) for g in groups]
    y_loc = [bdmm(a_rb[g], u_loc[g]) + bdmm(a_rk[g], v[g]) for g in groups]

    m0 = [state_ref[g] for g in groups]
    y_ref[0, 0] = jnp.concatenate([bdmm(q[g], m0[g]) + y_loc[g] for g in groups], axis=1)
    for g in groups:
        e_diag = jnp.where(eye, jnp.broadcast_to(jnp.exp(tot[g]), (L, GW)), 0.0)
        ed_hi, ed_lo = _split2(e_diag)
        w_sbs = _dot(ed_hi, seg_ones) + _dot(ed_lo, seg_ones)
        state_ref[g] = w_sbs * m0[g] + bdmm(gb[g], m0[g]) + hm[g]


def wkv_bidir(r, k, v, wz, az, k_k, k_a, r_k, *, n_groups=8):
    bsz, seq, dm = r.shape
    assert CHUNK == HEAD and seq % CHUNK == 0
    n_groups = min(n_groups, dm // GW)
    w = n_groups * GW
    assert dm % w == 0
    nc = seq // CHUNK

    def cidx(d, c):
        return c + d * (nc - 1 - 2 * c)

    x_spec = pl.BlockSpec((1, CHUNK, w), lambda d, b, j, c: (b, cidx(d, c), j))
    d_spec = pl.BlockSpec((1, 1, CHUNK, w), lambda d, b, j, c: (d, b, cidx(d, c), j))
    p_spec = pl.BlockSpec((1, w), lambda d, b, j, c: (0, j))
    out_sd = jax.ShapeDtypeStruct((2, bsz, seq, dm), F32)
    return pl.pallas_call(
        functools.partial(_wkv_kernel, n_groups=n_groups),
        grid=(2, bsz, dm // w, nc),
        in_specs=[x_spec, x_spec, x_spec, d_spec, d_spec, p_spec, p_spec, p_spec],
        out_specs=[d_spec, d_spec],
        out_shape=[out_sd, out_sd],
        scratch_shapes=[pltpu.VMEM((n_groups, HEAD, GW), F32)],
        compiler_params=_cparams(("parallel", "parallel", "parallel", "arbitrary")),
        name="wkv_bidir",
    )(r, k, v, wz, az, k_k.reshape(1, dm), k_a.reshape(1, dm), r_k.reshape(1, dm))


def _rms(x, g):
    return x * lax.rsqrt(jnp.mean(x * x, axis=-1, keepdims=True) + RMS_EPS) * g


def _sigmoid(x):
    return 1.0 / (1.0 + jnp.exp(-x))


def _rmsnorm_kernel(x_ref, g_ref, o_ref):
    o_ref[...] = _rms(x_ref[...], g_ref[...]).astype(o_ref.dtype)


def rmsnorm(x, g, *, out_dtype, tm=512):
    m, dm = x.shape
    tm = min(tm, m)
    return pl.pallas_call(
        _rmsnorm_kernel,
        grid=(m // tm,),
        in_specs=[pl.BlockSpec((tm, dm), lambda i: (i, 0)), pl.BlockSpec((1, dm), lambda i: (0, 0))],
        out_specs=pl.BlockSpec((tm, dm), lambda i: (i, 0)),
        out_shape=jax.ShapeDtypeStruct((m, dm), out_dtype),
        compiler_params=_cparams(("parallel",)),
        name="rmsnorm",
    )(x, g.reshape(1, dm))


def _rwkv_pre_kernel(x_ref, xp_ref, xn_ref, g_ref, mu_ref, *o_refs, ts):
    i = pl.program_id(1)
    g = g_ref[...]
    hn = _rms(x_ref[0], g)
    h_before = jnp.where(i == 0, 0.0, _rms(xp_ref[0], g)[7:8])
    h_after = jnp.where(i == pl.num_programs(1) - 1, 0.0, _rms(xn_ref[0], g)[0:1])
    row = lax.broadcasted_iota(jnp.int32, hn.shape, 0)
    prev = jnp.where(row == 0, h_before, pltpu.roll(hn, 1, 0))
    nxt = jnp.where(row == ts - 1, h_after, pltpu.roll(hn, ts - 1, 0))
    delta = 0.5 * (prev + nxt) - hn
    for n, o_ref in enumerate(o_refs):
        o_ref[0] = (hn + delta * mu_ref[n:n + 1, :]).astype(o_ref.dtype)


def rwkv_pre(h, g, mu, *, ts=256):
    bsz, seq, dm = h.shape
    ts = min(ts, seq)
    n_mix = mu.shape[0]
    sub = 8
    nsub = ts // sub
    x_spec = pl.BlockSpec((1, ts, dm), lambda b, i: (b, i, 0))
    return pl.pallas_call(
        functools.partial(_rwkv_pre_kernel, ts=ts),
        grid=(bsz, seq // ts),
        in_specs=[x_spec,
                  pl.BlockSpec((1, sub, dm), lambda b, i: (b, jnp.maximum(i * nsub - 1, 0), 0)),
                  pl.BlockSpec((1, sub, dm), lambda b, i: (b, jnp.minimum((i + 1) * nsub, seq // sub - 1), 0)),
                  pl.BlockSpec((1, dm), lambda b, i: (0, 0)),
                  pl.BlockSpec((n_mix, dm), lambda b, i: (0, 0))],
        out_specs=[x_spec] * n_mix,
        out_shape=[jax.ShapeDtypeStruct((bsz, seq, dm), BF16)] * n_mix,
        compiler_params=_cparams(("parallel", "parallel")),
        name="rwkv_pre",
    )(h, h, h, g.reshape(1, dm), mu)


def _lora_kernel(x_ref, w1_ref, w2_ref, b_ref, o_ref, *, act):
    hid = _dot(x_ref[...], w1_ref[0])
    if act == "tanh":
        hid = jnp.tanh(hid)
    elif act == "sigmoid":
        hid = _sigmoid(hid)
    o_ref[0] = (_dot(hid, w2_ref[0]) + b_ref[0]).astype(o_ref.dtype)


def lora(x, w1, w2, bias, *, act, tm=512):
    m, dm = x.shape
    nd, _, rank = w1.shape
    tm = min(tm, m)
    rpad = -rank % LANES
    w1 = jnp.pad(w1, ((0, 0), (0, 0), (0, rpad))).astype(BF16)
    w2 = jnp.pad(w2, ((0, 0), (0, rpad), (0, 0))).astype(BF16)
    rp = rank + rpad
    return pl.pallas_call(
        functools.partial(_lora_kernel, act=act),
        grid=(nd, m // tm),
        in_specs=[pl.BlockSpec((tm, dm), lambda d, i: (i, 0)),
                  pl.BlockSpec((1, dm, rp), lambda d, i: (d, 0, 0)),
                  pl.BlockSpec((1, rp, dm), lambda d, i: (d, 0, 0)),
                  pl.BlockSpec((1, 1, dm), lambda d, i: (d, 0, 0))],
        out_specs=pl.BlockSpec((1, tm, dm), lambda d, i: (d, i, 0)),
        out_shape=jax.ShapeDtypeStruct((nd, m, dm), F32),
        compiler_params=_cparams(("parallel", "parallel")),
        name="lora_" + act,
    )(x, w1, w2, bias.reshape(nd, 1, dm))


def _rwkv_post_kernel(y_ref, b_ref, g_ref, lw_ref, lb_ref, o_ref):
    dm = o_ref.shape[-1]
    seg = (lax.broadcasted_iota(jnp.int32, (GW, GW), 0) // HEAD
           == lax.broadcasted_iota(jnp.int32, (GW, GW), 1) // HEAD).astype(BF16)
    for s in range(dm // GW):
        sl = slice(s * GW, (s + 1) * GW)
        y = y_ref[0, :, sl] + y_ref[1, :, sl]
        mean = _dot(y, seg) * (1.0 / HEAD)
        yc = y - mean
        var = _dot(yc * yc, seg) * (1.0 / HEAD)
        yn = yc * lax.rsqrt(var + GN_EPS) * lw_ref[:, sl] + lb_ref[:, sl]
        out = yn + (b_ref[0, :, sl] + b_ref[1, :, sl])
        o_ref[:, sl] = (out * g_ref[0, :, sl]).astype(o_ref.dtype)


def rwkv_post(y, bonus, g, ln_w, ln_b, *, tm=256):
    _, m, dm = y.shape
    tm = min(tm, m)
    yspec = pl.BlockSpec((2, tm, dm), lambda i: (0, i, 0))
    pspec = pl.BlockSpec((1, dm), lambda i: (0, 0))
    return pl.pallas_call(
        _rwkv_post_kernel,
        grid=(m // tm,),
        in_specs=[yspec, yspec, pl.BlockSpec((1, tm, dm), lambda i: (0, i, 0)), pspec, pspec],
        out_specs=pl.BlockSpec((tm, dm), lambda i: (i, 0)),
        out_shape=jax.ShapeDtypeStruct((m, dm), BF16),
        compiler_params=_cparams(("parallel",)),
        name="rwkv_post",
    )(y, bonus, g, ln_w.reshape(1, dm), ln_b.reshape(1, dm))


def _ffn_kernel(be_ref, nu_ref, x_ref, wg_ref, wu_ref, wd_ref, *rest, has_res):
    del be_ref
    o_ref = rest[-1]
    f = pl.program_id(1)

    @pl.when(f == 0)
    def _():
        o_ref[...] = rest[0][...] if has_res else jnp.zeros_like(o_ref)

    @pl.when(pl.program_id(0) < nu_ref[0])
    def _():
        x = x_ref[...].astype(BF16)
        gate = _dot(x, wg_ref[0])
        up = _dot(x, wu_ref[0])
        o_ref[...] += _dot(gate * _sigmoid(gate) * up, wd_ref[0])


def ffn(x, blk_e, n_used, wg, wu, wd, *, res=None, tm, tf):
    p, dm = x.shape
    fdim = wg.shape[-1]
    assert p % tm == 0 and fdim % tf == 0
    nf = fdim // tf

    def blk(i, nu):
        return jnp.minimum(i, nu[0] - 1)

    def fblk(i, f, nu):
        return jnp.where(i < nu[0], f, nf - 1)

    row_spec = pl.BlockSpec((tm, dm), lambda i, f, be, nu: (i, 0))
    in_specs = [pl.BlockSpec((tm, dm), lambda i, f, be, nu: (blk(i, nu), 0)),
                pl.BlockSpec((1, dm, tf), lambda i, f, be, nu: (be[blk(i, nu)], 0, fblk(i, f, nu))),
                pl.BlockSpec((1, dm, tf), lambda i, f, be, nu: (be[blk(i, nu)], 0, fblk(i, f, nu))),
                pl.BlockSpec((1, tf, dm), lambda i, f, be, nu: (be[blk(i, nu)], fblk(i, f, nu), 0))]
    args = [x, wg, wu, wd]
    if res is not None:
        in_specs.append(row_spec)
        args.append(res)
    return pl.pallas_call(
        functools.partial(_ffn_kernel, has_res=res is not None),
        grid_spec=pltpu.PrefetchScalarGridSpec(
            num_scalar_prefetch=2, grid=(p // tm, nf),
            in_specs=in_specs, out_specs=row_spec),
        out_shape=jax.ShapeDtypeStruct((p, dm), F32),
        compiler_params=_cparams(("parallel", "arbitrary")),
        name="ffn",
    )(blk_e, n_used, *args)


def _rope_kernel(q_ref, k_ref, v_ref, pos_ref, inv_ref, qo_ref, ko_ref, vo_ref):
    tm, dm = qo_ref.shape
    ang = pos_ref[...].astype(F32) * inv_ref[...]
    cos = jnp.cos(ang)
    sin = jnp.sin(ang)
    lane = lax.broadcasted_iota(jnp.int32, (tm, LANES), 1)
    first = (lane % HEAD) < HEAD // 2
    sin_signed = jnp.where(first, -sin, sin)
    ones_col = jnp.where(lane == 0, 1.0, 0.0).astype(vo_ref.dtype)
    for s in range(dm // LANES):
        sl = slice(s * LANES, (s + 1) * LANES)
        for src, dst, scale in ((q_ref, qo_ref, HEAD ** -0.5 * LOG2E), (k_ref, ko_ref, 1.0)):
            t = src[:, sl]
            partner = jnp.where(first, pltpu.roll(t, LANES - HEAD // 2, 1), pltpu.roll(t, HEAD // 2, 1))
            dst[:, sl] = ((t * cos + partner * sin_signed) * scale).astype(dst.dtype)
        vo_ref[:, 2 * s * LANES:(2 * s + 1) * LANES] = v_ref[:, sl].astype(vo_ref.dtype)
        vo_ref[:, (2 * s + 1) * LANES:(2 * s + 2) * LANES] = ones_col


def rope_qkv(qkv, positions, *, tm=256):
    m, dm3 = qkv.shape
    dm = dm3 // 3
    assert 2 * HEAD == LANES
    tm = min(tm, m)
    half = HEAD // 2
    inv = ROPE_THETA ** (-(2.0 * (jnp.arange(LANES) % half)).astype(F32) / HEAD)
    specs = [pl.BlockSpec((tm, dm), lambda i, c=c: (i, c)) for c in range(3)]
    o_spec = pl.BlockSpec((tm, dm), lambda i: (i, 0))
    return pl.pallas_call(
        _rope_kernel,
        grid=(m // tm,),
        in_specs=specs + [pl.BlockSpec((tm, 1), lambda i: (i, 0)), pl.BlockSpec((1, LANES), lambda i: (0, 0))],
        out_specs=[o_spec, o_spec, pl.BlockSpec((tm, 2 * dm), lambda i: (i, 0))],
        out_shape=[jax.ShapeDtypeStruct((m, dm), BF16)] * 2 + [jax.ShapeDtypeStruct((m, 2 * dm), BF16)],
        compiler_params=_cparams(("parallel",)),
        name="rope_qkv",
    )(qkv, qkv, qkv, positions.reshape(m, 1), inv.reshape(1, LANES))


def _attn_kernel(q_ref, k_ref, v_ref, lq1_ref, lk1_ref, lq2_ref, lk2_ref, sub_ref, o_ref,
                 m_ref, acc_ref, s_ref, *, lambda_init, tk):
    tq, hw = q_ref.shape
    seq = k_ref.shape[0]
    m_ref[...] = jnp.full_like(m_ref, -jnp.inf)
    acc_ref[...] = jnp.zeros_like(acc_ref)
    q = q_ref[...]
    lane = lax.broadcasted_iota(jnp.int32, q.shape, 1)
    qs = [jnp.where((lane >= c * HEAD) & (lane < (c + 1) * HEAD), q, jnp.zeros((), q.dtype)) for c in range(2)]

    def rows(j):
        return pl.ds(pl.multiple_of(j * tk, tk), tk)

    def scores(j, slot):
        ks = k_ref[rows(j), :]
        for c in range(2):
            s_ref[slot, c] = _dot_nt(qs[c], ks)

    def accumulate(j, slot):
        vs = v_ref[rows(j), :]
        for c in range(2):
            s = s_ref[slot, c]
            m_prev = m_ref[c]
            m_new = jnp.maximum(m_prev, jnp.max(s, axis=1, keepdims=True))
            alpha = jnp.exp2(m_prev - m_new)
            p = jnp.exp2(s - jnp.concatenate([m_new] * (tk // hw), axis=1))
            acc_ref[c] = acc_ref[c] * jnp.concatenate([alpha, alpha], axis=1) + _dot(p, vs)
            m_ref[c] = m_new

    def body(jj, carry):
        scores(2 * jj + 1, 1)
        accumulate(2 * jj, 0)
        scores(2 * jj + 2, 0)
        accumulate(2 * jj + 1, 1)
        return carry

    n_pairs = seq // (2 * tk)
    scores(0, 0)
    lax.fori_loop(0, n_pairs - 1, body, 0)
    scores(2 * n_pairs - 1, 1)
    accumulate(2 * n_pairs - 2, 0)
    accumulate(2 * n_pairs - 1, 1)
    lam = (jnp.exp(jnp.sum(lq1_ref[...] * lk1_ref[...], keepdims=True))
           - jnp.exp(jnp.sum(lq2_ref[...] * lk2_ref[...], keepdims=True)) + lambda_init)
    a1 = acc_ref[0]
    a2 = acc_ref[1]
    o = a1[:, :hw] / a1[:, hw:hw + 1] - lam * (a2[:, :hw] / a2[:, hw:hw + 1])
    o = o * lax.rsqrt(jnp.mean(o * o, axis=-1, keepdims=True) + SUBLN_EPS) * sub_ref[...] * (1.0 - lambda_init)
    o_ref[...] = o.astype(o_ref.dtype)


def diff_attn(q, k, v1, lq1, lk1, lq2, lk2, subln, lambda_init, *, bsz, tq=512, tk=512):
    m, dm = q.shape
    seq = m // bsz
    hw = 2 * HEAD
    tq, tk = min(tq, seq), min(tk, seq // 2)
    assert seq % tq == 0 and seq % (2 * tk) == 0
    nq = seq // tq
    q_spec = pl.BlockSpec((tq, hw), lambda b, h, i: (b * nq + i, h))
    l_spec = pl.BlockSpec((1, HEAD), lambda b, h, i: (0, 0))
    return pl.pallas_call(
        functools.partial(_attn_kernel, lambda_init=lambda_init, tk=tk),
        grid=(bsz, dm // hw, nq),
        in_specs=[q_spec, pl.BlockSpec((seq, hw), lambda b, h, i: (b, h)),
                  pl.BlockSpec((seq, 2 * hw), lambda b, h, i: (b, h)), l_spec, l_spec, l_spec, l_spec,
                  pl.BlockSpec((1, hw), lambda b, h, i: (0, 0))],
        out_specs=q_spec,
        out_shape=jax.ShapeDtypeStruct((m, dm), BF16),
        scratch_shapes=[pltpu.VMEM((2, tq, hw), F32), pltpu.VMEM((2, tq, 2 * hw), F32),
                        pltpu.VMEM((2, 2, tq, tk), F32)],
        compiler_params=_cparams(("parallel", "parallel", "parallel")),
        name="diff_attn",
    )(q, k, v1, lq1.reshape(1, HEAD), lk1.reshape(1, HEAD), lq2.reshape(1, HEAD), lk2.reshape(1, HEAD),
      subln.reshape(1, hw))


def _router_kernel(h_ref, g_ref, wr_ref, hn_ref, e_ref, gate_ref, rank_ref, cnt_ref, *, n_exp):
    tm = h_ref.shape[0]

    @pl.when(pl.program_id(0) == 0)
    def _():
        cnt_ref[...] = jnp.zeros_like(cnt_ref)

    hn = _rms(h_ref[...], g_ref[...])
    hn_ref[...] = hn
    lane = lax.broadcasted_iota(jnp.int32, (tm, LANES), 1)
    logits = jnp.where(lane < n_exp, _dot_f32(hn, wr_ref[...]), -jnp.inf)
    m1 = jnp.max(logits, axis=1, keepdims=True)
    i1 = jnp.min(jnp.where(logits == m1, lane, LANES), axis=1, keepdims=True)
    rest = jnp.where(lane == i1, -jnp.inf, logits)
    m2 = jnp.max(rest, axis=1, keepdims=True)
    i2 = jnp.min(jnp.where(rest == m2, lane, LANES), axis=1, keepdims=True)
    e21 = jnp.exp(m2 - m1)
    g1 = 1.0 / (1.0 + e21)
    g2 = e21 / (1.0 + e21)
    oh1 = lane == i1
    oh2 = lane == i2
    both = jnp.where(oh1 | oh2, 1.0, 0.0)
    earlier = (lax.broadcasted_iota(jnp.int32, (tm, tm), 0) > lax.broadcasted_iota(jnp.int32, (tm, tm), 1))
    cum = _dot(earlier.astype(BF16), both) + cnt_ref[...]
    r1 = jnp.sum(jnp.where(oh1, cum, 0.0), axis=1, keepdims=True)
    r2 = jnp.sum(jnp.where(oh2, cum, 0.0), axis=1, keepdims=True)
    cnt_ref[...] += jnp.sum(both, axis=0, keepdims=True)
    e_ref[...] = jnp.where(lane == 0, i1, jnp.where(lane == 1, i2, 0))
    gate_ref[...] = jnp.where(lane == 0, g1, jnp.where(lane == 1, g2, 0.0))
    rank_ref[...] = jnp.where(lane == 0, r1, jnp.where(lane == 1, r2, 0.0)).astype(jnp.int32)


def moe_route(h, g, w_router, *, tm=512):
    m, dm = h.shape
    n_exp = w_router.shape[1]
    tm = min(tm, m)
    wr = jnp.pad(w_router, ((0, 0), (0, LANES - n_exp)))
    row = lambda dt: jax.ShapeDtypeStruct((m, LANES), dt)
    lspec = pl.BlockSpec((tm, LANES), lambda i: (i, 0))
    return pl.pallas_call(
        functools.partial(_router_kernel, n_exp=n_exp),
        grid=(m // tm,),
        in_specs=[pl.BlockSpec((tm, dm), lambda i: (i, 0)), pl.BlockSpec((1, dm), lambda i: (0, 0)),
                  pl.BlockSpec((dm, LANES), lambda i: (0, 0))],
        out_specs=[pl.BlockSpec((tm, dm), lambda i: (i, 0)), lspec, lspec, lspec,
                   pl.BlockSpec((1, LANES), lambda i: (0, 0))],
        out_shape=[jax.ShapeDtypeStruct((m, dm), F32), row(jnp.int32), row(F32), row(jnp.int32),
                   jax.ShapeDtypeStruct((1, LANES), F32)],
        compiler_params=_cparams(("arbitrary",)),
        name="moe_route",
    )(h, g.reshape(1, dm), wr)


def _dest_kernel(e_ref, rank_ref, cnt_ref, dest_ref, be_ref, *, n_exp, bm):
    tm = e_ref.shape[0]
    nb = be_ref.shape[0]
    cnt = cnt_ref[...]
    pcnt = jnp.floor((cnt + (bm - 1)) * (1.0 / bm)) * bm
    upper = (lax.broadcasted_iota(jnp.int32, (LANES, LANES), 0)
             <= lax.broadcasted_iota(jnp.int32, (LANES, LANES), 1)).astype(F32)
    pends = _dot_f32(jnp.broadcast_to(pcnt, (8, LANES)), upper)[0:1]
    pstart = pends - pcnt
    lane = lax.broadcasted_iota(jnp.int32, (tm, LANES), 1)
    e = e_ref[...]
    d1 = jnp.sum(jnp.where(lane == e[:, 0:1], pstart, 0.0), axis=1, keepdims=True)
    d2 = jnp.sum(jnp.where(lane == e[:, 1:2], pstart, 0.0), axis=1, keepdims=True)
    dest = jnp.where(lane == 0, d1, jnp.where(lane == 1, d2, 0.0)).astype(jnp.int32) + rank_ref[...]
    dest_ref[...] = dest
    blk_lane = lax.broadcasted_iota(jnp.int32, (nb, LANES), 1)
    blk_row = lax.broadcasted_iota(jnp.int32, (nb, LANES), 0).astype(F32) * bm
    ended = jnp.where((blk_lane < n_exp) & (pends <= blk_row), 1.0, 0.0)
    be = jnp.minimum(jnp.sum(ended, axis=1, keepdims=True), n_exp - 1.0)
    n_used = jnp.sum(jnp.where(blk_lane == n_exp - 1, pends, 0.0), axis=1, keepdims=True) * (1.0 / bm)
    be_ref[...] = jnp.where(blk_lane == 1, n_used, be).astype(jnp.int32)


def moe_dest(e, rank, cnt, *, n_exp, bm, nblk, tm=512):
    m = e.shape[0]
    tm = min(tm, m)
    nb = -(-nblk // 8) * 8
    lspec = pl.BlockSpec((tm, LANES), lambda i: (i, 0))
    return pl.pallas_call(
        functools.partial(_dest_kernel, n_exp=n_exp, bm=bm),
        grid=(m // tm,),
        in_specs=[lspec, lspec, pl.BlockSpec((1, LANES), lambda i: (0, 0))],
        out_specs=[lspec, pl.BlockSpec((nb, LANES), lambda i: (0, 0))],
        out_shape=[jax.ShapeDtypeStruct((m, LANES), jnp.int32), jax.ShapeDtypeStruct((nb, LANES), jnp.int32)],
        compiler_params=_cparams(("arbitrary",)),
        name="moe_dest",
    )(e, rank, cnt)


def _row_copy(src, dst, sem):
    return pltpu.make_async_copy(src, dst, sem)


def _dispatch_kernel(dest_ref, hn_ref, xs_in_ref, xs_ref, sem):
    del xs_in_ref
    tm = hn_ref.shape[0]
    base = pl.program_id(0) * tm

    def issue(r, carry):
        for s in range(TOP_K):
            dst = dest_ref[TOP_K * (base + r) + s]
            _row_copy(hn_ref.at[pl.ds(r, 1)], xs_ref.at[pl.ds(dst, 1)], sem).start()
        return carry

    lax.fori_loop(0, tm, issue, 0)

    def drain(r, carry):
        for s in range(TOP_K):
            _row_copy(hn_ref.at[pl.ds(0, 1)], xs_ref.at[pl.ds(0, 1)], sem).wait()
        return carry

    lax.fori_loop(0, tm, drain, 0)


def moe_dispatch(dest_flat, hn, n_rows, *, tm=256):
    m, dm = hn.shape
    tm = min(tm, m)
    xs0 = jnp.zeros((n_rows, dm), hn.dtype)
    return pl.pallas_call(
        _dispatch_kernel,
        grid_spec=pltpu.PrefetchScalarGridSpec(
            num_scalar_prefetch=1, grid=(m // tm,),
            in_specs=[pl.BlockSpec((tm, dm), lambda i, d: (i, 0)), pl.BlockSpec(memory_space=pl.ANY)],
            out_specs=pl.BlockSpec(memory_space=pl.ANY),
            scratch_shapes=[pltpu.SemaphoreType.DMA(())]),
        out_shape=jax.ShapeDtypeStruct((n_rows, dm), hn.dtype),
        input_output_aliases={2: 0},
        compiler_params=_cparams(("arbitrary",)),
        name="moe_dispatch",
    )(dest_flat, hn, xs0)


def _combine_kernel(dest_ref, yb_ref, gate_ref, res_ref, o_ref, buf_ref, sem):
    tm = res_ref.shape[0]
    base = pl.program_id(0) * tm

    def issue(r, carry):
        for s in range(TOP_K):
            src = dest_ref[TOP_K * (base + r) + s]
            _row_copy(yb_ref.at[pl.ds(src, 1)], buf_ref.at[s, pl.ds(r, 1)], sem).start()
        return carry

    lax.fori_loop(0, tm, issue, 0)

    def drain(r, carry):
        for s in range(TOP_K):
            _row_copy(yb_ref.at[pl.ds(0, 1)], buf_ref.at[s, pl.ds(0, 1)], sem).wait()
        return carry

    lax.fori_loop(0, tm, drain, 0)
    gate = gate_ref[...]
    o_ref[...] = res_ref[...] + (gate[:, 0:1] * buf_ref[0] + gate[:, 1:2] * buf_ref[1])


def moe_combine(dest_flat, yb, gates, res, *, tm=256):
    m, dm = res.shape
    tm = min(tm, m)
    row_spec = pl.BlockSpec((tm, dm), lambda i, d: (i, 0))
    return pl.pallas_call(
        _combine_kernel,
        grid_spec=pltpu.PrefetchScalarGridSpec(
            num_scalar_prefetch=1, grid=(m // tm,),
            in_specs=[pl.BlockSpec(memory_space=pl.ANY), pl.BlockSpec((tm, LANES), lambda i, d: (i, 0)), row_spec],
            out_specs=row_spec,
            scratch_shapes=[pltpu.VMEM((TOP_K, tm, dm), F32), pltpu.SemaphoreType.DMA(())]),
        out_shape=jax.ShapeDtypeStruct((m, dm), F32),
        compiler_params=_cparams(("arbitrary",)),
        name="moe_combine",
    )(dest_flat, yb, gates, res)


def moe_layer(h, g, w_router, wg, wu, wd):
    m, dm = h.shape
    n_exp = w_router.shape[1]
    bm = 1024 if m * TOP_K >= 8 * 1024 else 128
    n_rows = m * TOP_K + n_exp * bm
    nblk = n_rows // bm
    hn, e, gates, rank, cnt = moe_route(h, g, w_router)
    dest, be = moe_dest(e, rank, cnt, n_exp=n_exp, bm=bm, nblk=nblk)
    dest_flat = dest[:, :TOP_K].reshape(-1)
    xs = moe_dispatch(dest_flat, hn, n_rows)
    tf = 256 if wg.shape[-1] % 256 == 0 else LANES
    yb = ffn(xs, be[:nblk, 0], be[0, 1:2], wg, wu, wd, tm=bm, tf=tf)
    return moe_combine(dest_flat, yb, gates, h)


def rwkv_layer(h, bsz, g_norm, mu, w_rkv, g1, g2, w0, w1, w2, a0, a1, a2, k_k, k_a, r_k, ln_w, ln_b, w_o):
    m, dm = h.shape
    seq = m // bsz
    xr, xk, xv, xw, xa, xg = (t.reshape(m, dm) for t in rwkv_pre(h.reshape(bsz, seq, dm), g_norm, mu))
    w_rkv = w_rkv.astype(BF16)
    r, k, v = (matmul(xi, w_rkv, n=dm, col_off=n * dm).reshape(bsz, seq, dm)
               for n, xi in enumerate((xr, xk, xv)))
    gate = lora(xg, g1[None], g2[None], jnp.zeros((1, dm), F32), act="sigmoid")
    wz = lora(xw, w1, w2, w0, act="tanh").reshape(2, bsz, seq, dm)
    az = lora(xa, a1, a2, a0, act="none").reshape(2, bsz, seq, dm)
    y, bonus = wkv_bidir(r, k, v, wz, az, k_k, k_a, r_k)
    z = rwkv_post(y.reshape(2, m, dm), bonus.reshape(2, m, dm), gate, ln_w, ln_b)
    return matmul(z, w_o.astype(BF16), res=h)


def attn_layer(h, bsz, positions, g_norm, w_qkv, lq1, lk1, lq2, lk2, subln, w_o, lambda_init):
    hn = rmsnorm(h, g_norm, out_dtype=BF16)
    qkv = matmul(hn, w_qkv.astype(BF16))
    q, k, v = rope_qkv(qkv, positions)
    o = diff_attn(q, k, v, lq1, lk1, lq2, lk2, subln, lambda_init, bsz=bsz)
    return matmul(o, w_o.astype(BF16), res=h)


def dense_ffn_layer(h, g_norm, wg, wu, wd):
    m = h.shape[0]
    hn = rmsnorm(h, g_norm, out_dtype=BF16)
    tm = min(512, m)
    tf = 512 if wg.shape[-1] % 512 == 0 else LANES
    return ffn(hn, jnp.zeros((m // tm,), jnp.int32), jnp.full((1,), m // tm, jnp.int32),
               wg.astype(BF16)[None], wu.astype(BF16)[None], wd.astype(BF16)[None], res=h, tm=tm, tf=tf)


def kernel(x, positions, norm_mix, norm_ffn, norm_final, rw_mu, rw_w_rkv, rw_g1, rw_g2, rw_w0, rw_w1, rw_w2, rw_a0, rw_a1, rw_a2, rw_kk, rw_ka, rw_rk, rw_ln_w, rw_ln_b, rw_w_o, da_w_qkv, da_lq1, da_lk1, da_lq2, da_lk2, da_subln, da_w_o, ff_wg, ff_wu, ff_wd, moe_router, moe_wg, moe_wu, moe_wd):
    bsz, seq, dm = x.shape
    depth = norm_mix.shape[0]
    h = x.reshape(bsz * seq, dm)
    for i in range(depth):
        j = i // 2
        if i % 2 == 0:
            h = rwkv_layer(h, bsz, norm_mix[i], rw_mu[j], rw_w_rkv[j], rw_g1[j], rw_g2[j], rw_w0[j], rw_w1[j],
                           rw_w2[j], rw_a0[j], rw_a1[j], rw_a2[j], rw_kk[j], rw_ka[j], rw_rk[j],
                           rw_ln_w[j], rw_ln_b[j], rw_w_o[j])
            h = dense_ffn_layer(h, norm_ffn[i], ff_wg[j], ff_wu[j], ff_wd[j])
        else:
            lambda_init = 0.8 - 0.6 * math.exp(-0.3 * i)
            h = attn_layer(h, bsz, positions, norm_mix[i], da_w_qkv[j], da_lq1[j], da_lk1[j], da_lq2[j],
                           da_lk2[j], da_subln[j], da_w_o[j], lambda_init)
            h = moe_layer(h, norm_ffn[i], moe_router[j], moe_wg[j], moe_wu[j], moe_wd[j])
    return rmsnorm(h, norm_final, out_dtype=x.dtype).reshape(bsz, seq, dm)
```

```python
import functools
import math

import jax
import jax.numpy as jnp
from jax import lax
from jax.experimental import pallas as pl
from jax.experimental.pallas import tpu as pltpu

F32 = jnp.float32
BF16 = jnp.bfloat16

RMS_EPS = 1e-6
HEAD = 64
GN_EPS = HEAD * 1e-5
ROPE_THETA = 10000.0
SUBLN_EPS = 1e-5
LOG2E = math.log2(math.e)
TOP_K = 2
LANES = 128
CHUNK = 64
GROUP = 4
GW = GROUP * HEAD
VMEM_LIMIT = 56 * 1024 * 1024


def _cparams(sem, vmem=VMEM_LIMIT):
    return pltpu.CompilerParams(dimension_semantics=sem, vmem_limit_bytes=vmem)


def _dot(a, b):
    return jnp.dot(a.astype(BF16), b.astype(BF16), preferred_element_type=F32)


def _dot_f32(a, b):
    return jnp.dot(a, b, preferred_element_type=F32, precision=lax.Precision.HIGHEST)


def _split2(x):
    hi = x.astype(BF16)
    return hi, (x - hi.astype(F32)).astype(BF16)


def _dot_nt(a, b):
    return lax.dot_general(a.astype(BF16), b.astype(BF16), (((1,), (1,)), ((), ())),
                           preferred_element_type=F32)


def _dot_tn(a, b):
    return lax.dot_general(a.astype(BF16), b.astype(BF16), (((0,), (0,)), ((), ())),
                           preferred_element_type=F32)


def _mm_kernel(*refs, has_bias, has_res):
    a_ref, b_ref = refs[0], refs[1]
    o_ref = refs[-1]
    acc = _dot(a_ref[...], b_ref[...])
    pos = 2
    if has_bias:
        acc = acc + refs[pos][...]
        pos += 1
    if has_res:
        acc = acc + refs[pos][...]
    o_ref[...] = acc.astype(o_ref.dtype)


def matmul(a, b, *, n=None, col_off=0, bias=None, res=None, out_dtype=F32, tm=1024, tn=512):
    m, k = a.shape
    n = b.shape[1] if n is None else n
    tm, tn = min(tm, m), min(tn, n)
    assert m % tm == 0 and n % tn == 0 and col_off % tn == 0
    off = col_off // tn
    in_specs = [pl.BlockSpec((tm, k), lambda i, j: (i, 0)),
                pl.BlockSpec((k, tn), lambda i, j: (0, j + off))]
    args = [a, b]
    if bias is not None:
        in_specs.append(pl.BlockSpec((1, tn), lambda i, j: (0, j)))
        args.append(bias.reshape(1, n))
    if res is not None:
        in_specs.append(pl.BlockSpec((tm, tn), lambda i, j: (i, j)))
        args.append(res)
    return pl.pallas_call(
        functools.partial(_mm_kernel, has_bias=bias is not None, has_res=res is not None),
        grid=(m // tm, n // tn),
        in_specs=in_specs,
        out_specs=pl.BlockSpec((tm, tn), lambda i, j: (i, j)),
        out_shape=jax.ShapeDtypeStruct((m, n), out_dtype),
        compiler_params=_cparams(("parallel", "parallel")),
        name="matmul",
    )(*args)


def _wkv_kernel(r_ref, k_ref, v_ref, hw_ref, ha_ref, w2_ref, a2_ref, w0_ref, a0_ref, kk_ref, ka_ref, rk_ref,
                y_ref, bonus_ref, state_ref, *, n_groups):
    L = CHUNK
    d = pl.program_id(0)
    sgn = 1 - 2 * d

    @pl.when(pl.program_id(3) == 0)
    def _():
        state_ref[...] = jnp.zeros_like(state_ref)

    row = lax.broadcasted_iota(jnp.int32, (L, GW), 0)
    col = lax.broadcasted_iota(jnp.int32, (L, GW), 1) % L
    diff = (row - col) * sgn
    strict = diff > 0
    incl = diff >= 0
    eye = diff == 0
    level_masks = []
    s = 1
    while s < L:
        level_masks.append(strict & ((row // (2 * s)) == (col // (2 * s))) & ((row // s) != (col // s)))
        s *= 2
    bd_mask = (lax.broadcasted_iota(jnp.int32, (GROUP * L, GW), 0) // L
               == lax.broadcasted_iota(jnp.int32, (GROUP * L, GW), 1) // HEAD)
    seg_ones = bd_mask.astype(BF16)
    tri = (lax.broadcasted_iota(jnp.int32, (L, L), 0) - lax.broadcasted_iota(jnp.int32, (L, L), 1)) * sgn >= 0
    tri = tri.astype(F32)

    def bd(x):
        xb = x.astype(BF16)
        return jnp.where(bd_mask, jnp.concatenate([xb] * GROUP, axis=0), jnp.zeros((), BF16))

    def mm(x, yb):
        return jnp.dot(x.astype(BF16), yb, preferred_element_type=F32)

    def bdmm(x, y):
        return mm(x, bd(y))

    def fold(full):
        fm = jnp.where(bd_mask, full, 0.0)
        out = fm[0:HEAD]
        for j in range(1, GROUP):
            out = out + fm[j * HEAD:(j + 1) * HEAD]
        return out

    groups = range(n_groups)
    sls = [slice(g * GW, (g + 1) * GW) for g in groups]
    r = [r_ref[0, :, sl].astype(F32) for sl in sls]
    k = [k_ref[0, :, sl].astype(F32) for sl in sls]
    v = [v_ref[0, :, sl].astype(F32) for sl in sls]

    wz_all = _dot(hw_ref[0, 0], w2_ref[0]) + w0_ref[0]
    az_all = _dot(ha_ref[0, 0], a2_ref[0]) + a0_ref[0]
    lw, a_s = [], []
    for sl in sls:
        u = -wz_all[:, sl]
        softplus = jnp.maximum(u, 0.0) + jnp.log(1.0 + jnp.exp(-jnp.abs(u)))
        lw.append(-jnp.exp(-softplus - 0.5))
        a_s.append(1.0 / (1.0 + jnp.exp(-az_all[:, sl])))
    split = [_split2(x) for x in lw]
    c = [_dot(tri, hi) + _dot(tri, lo) for hi, lo in split]
    kkr = [k[g] * kk_ref[:, sls[g]] for g in groups]
    ss = [_dot(x * x, seg_ones) for x in kkr]
    kk = [kkr[g] / jnp.maximum(jnp.sqrt(ss[g]), 1e-12) for g in groups]
    kd = [k[g] * (1.0 + (a_s[g] - 1.0) * ka_ref[:, sls[g]]) for g in groups]
    b_vec = [kk[g] * a_s[g] for g in groups]
    rkd = [_dot(r[g] * kd[g] * rk_ref[:, sls[g]], seg_ones) for g in groups]
    bonus_ref[0, 0] = jnp.concatenate([rkd[g] * v[g] for g in groups], axis=1).astype(bonus_ref.dtype)

    tot = [jnp.sum(x, axis=0, keepdims=True) for x in lw]
    e_nc = [jnp.exp(-x) for x in c]
    e_tc = [jnp.exp(tot[g] - c[g]) for g in groups]
    at = [-kk[g] * jnp.exp(c[g] - lw[g]) for g in groups]
    rt = [r[g] * jnp.exp(c[g]) for g in groups]
    bt = [b_vec[g] * e_nc[g] for g in groups]
    kt = [kd[g] * e_nc[g] for g in groups]
    bh = [b_vec[g] * e_tc[g] for g in groups]
    kh = [kd[g] * e_tc[g] for g in groups]

    ar = [jnp.concatenate([at[g], rt[g]], axis=0) for g in groups]
    pb = [_dot_nt(ar[g], bd(bt[g])) for g in groups]
    pk = [_dot_nt(ar[g], bd(kt[g])) for g in groups]
    a_ab = [jnp.where(strict, x[:L], 0.0) for x in pb]
    a_rb = [jnp.where(incl, x[L:], 0.0) for x in pb]
    a_ak = [jnp.where(strict, x[:L], 0.0) for x in pk]
    a_rk = [jnp.where(incl, x[L:], 0.0) for x in pk]
    v_bd = [bd(x) for x in v]
    x_loc = [mm(a_ak[g], v_bd[g]) for g in groups]

    t_inv = [jnp.where(eye, 1.0, 0.0) + jnp.where(level_masks[0], a, 0.0) for a in a_ab]
    for m in level_masks[1:]:
        t1 = [bdmm(t_inv[g], jnp.where(m, a_ab[g], 0.0)) for g in groups]
        t_inv = [t_inv[g] + bdmm(t1[g], t_inv[g]) for g in groups]

    u_loc = [bdmm(t_inv[g], x_loc[g]) for g in groups]
    ta = [bdmm(t_inv[g], at[g]) for g in groups]
    gb = [fold(_dot_tn(bh[g], ta[g])) for g in groups]
    hm = [fold(_dot_tn(jnp.concatenate([bh[g], kh[g]], axis=0), jnp.concatenate([u_loc[g], v[g]], axis=0)))
          for g in groups]
    q = [rt[g] + bdmm(a_rb[g], ta[g]) for g in groups]
    y_loc = [bdmm(a_rb[g], u_loc[g]) + mm(a_rk[g], v_bd[g]) for g in groups]

    m0 = [state_ref[g] for g in groups]
    m0_bd = [bd(x) for x in m0]
    y_ref[0, 0] = jnp.concatenate([mm(q[g], m0_bd[g]) + y_loc[g] for g in groups], axis=1).astype(y_ref.dtype)
    for g in groups:
        e_diag = jnp.where(eye, jnp.broadcast_to(jnp.exp(tot[g]), (L, GW)), 0.0)
        ed_hi, ed_lo = _split2(e_diag)
        w_sbs = _dot(ed_hi, seg_ones) + _dot(ed_lo, seg_ones)
        state_ref[g] = w_sbs * m0[g] + mm(gb[g], m0_bd[g]) + hm[g]


def wkv_bidir(r, k, v, hw, ha, w2, a2, w0, a0, k_k, k_a, r_k, *, n_groups=8):
    bsz, seq, dm = r.shape
    rank = hw.shape[-1]
    assert CHUNK == HEAD and seq % CHUNK == 0
    n_groups = min(n_groups, dm // GW)
    w = n_groups * GW
    assert dm % w == 0
    nc = seq // CHUNK

    def cidx(d, c):
        return c + d * (nc - 1 - 2 * c)

    x_spec = pl.BlockSpec((1, CHUNK, w), lambda d, b, j, c: (b, cidx(d, c), j))
    d_spec = pl.BlockSpec((1, 1, CHUNK, w), lambda d, b, j, c: (d, b, cidx(d, c), j))
    p_spec = pl.BlockSpec((1, w), lambda d, b, j, c: (0, j))
    h_spec = pl.BlockSpec((1, 1, CHUNK, rank), lambda d, b, j, c: (d, b, cidx(d, c), 0))
    w2_spec = pl.BlockSpec((1, rank, w), lambda d, b, j, c: (d, 0, j))
    b_spec = pl.BlockSpec((1, 1, w), lambda d, b, j, c: (d, 0, j))
    out_sd = jax.ShapeDtypeStruct((2, bsz, seq, dm), BF16)
    return pl.pallas_call(
        functools.partial(_wkv_kernel, n_groups=n_groups),
        grid=(2, bsz, dm // w, nc),
        in_specs=[x_spec, x_spec, x_spec, h_spec, h_spec, w2_spec, w2_spec, b_spec, b_spec,
                  p_spec, p_spec, p_spec],
        out_specs=[d_spec, d_spec],
        out_shape=[out_sd, out_sd],
        scratch_shapes=[pltpu.VMEM((n_groups, HEAD, GW), F32)],
        compiler_params=_cparams(("parallel", "parallel", "parallel", "arbitrary")),
        name="wkv_bidir",
    )(r, k, v, hw, ha, w2, a2, w0.reshape(2, 1, dm), a0.reshape(2, 1, dm),
      k_k.reshape(1, dm), k_a.reshape(1, dm), r_k.reshape(1, dm))


def _rms(x, g):
    return x * lax.rsqrt(jnp.mean(x * x, axis=-1, keepdims=True) + RMS_EPS) * g


def _sigmoid(x):
    return 1.0 / (1.0 + jnp.exp(-x))


def _rmsnorm_kernel(x_ref, g_ref, o_ref):
    o_ref[...] = _rms(x_ref[...], g_ref[...]).astype(o_ref.dtype)


def rmsnorm(x, g, *, out_dtype, tm=512):
    m, dm = x.shape
    tm = min(tm, m)
    return pl.pallas_call(
        _rmsnorm_kernel,
        grid=(m // tm,),
        in_specs=[pl.BlockSpec((tm, dm), lambda i: (i, 0)), pl.BlockSpec((1, dm), lambda i: (0, 0))],
        out_specs=pl.BlockSpec((tm, dm), lambda i: (i, 0)),
        out_shape=jax.ShapeDtypeStruct((m, dm), out_dtype),
        compiler_params=_cparams(("parallel",)),
        name="rmsnorm",
    )(x, g.reshape(1, dm))


def _rwkv_pre_kernel(x_ref, xp_ref, xn_ref, g_ref, mu_ref, *o_refs, ts):
    i = pl.program_id(1)
    g = g_ref[...]
    hn = _rms(x_ref[0], g)
    h_before = jnp.where(i == 0, 0.0, _rms(xp_ref[0], g)[7:8])
    h_after = jnp.where(i == pl.num_programs(1) - 1, 0.0, _rms(xn_ref[0], g)[0:1])
    row = lax.broadcasted_iota(jnp.int32, hn.shape, 0)
    prev = jnp.where(row == 0, h_before, pltpu.roll(hn, 1, 0))
    nxt = jnp.where(row == ts - 1, h_after, pltpu.roll(hn, ts - 1, 0))
    delta = 0.5 * (prev + nxt) - hn
    for n, o_ref in enumerate(o_refs):
        o_ref[0] = (hn + delta * mu_ref[n:n + 1, :]).astype(o_ref.dtype)


def rwkv_pre(h, g, mu, *, ts=256):
    bsz, seq, dm = h.shape
    ts = min(ts, seq)
    n_mix = mu.shape[0]
    sub = 8
    nsub = ts // sub
    x_spec = pl.BlockSpec((1, ts, dm), lambda b, i: (b, i, 0))
    return pl.pallas_call(
        functools.partial(_rwkv_pre_kernel, ts=ts),
        grid=(bsz, seq // ts),
        in_specs=[x_spec,
                  pl.BlockSpec((1, sub, dm), lambda b, i: (b, jnp.maximum(i * nsub - 1, 0), 0)),
                  pl.BlockSpec((1, sub, dm), lambda b, i: (b, jnp.minimum((i + 1) * nsub, seq // sub - 1), 0)),
                  pl.BlockSpec((1, dm), lambda b, i: (0, 0)),
                  pl.BlockSpec((n_mix, dm), lambda b, i: (0, 0))],
        out_specs=[x_spec] * n_mix,
        out_shape=[jax.ShapeDtypeStruct((bsz, seq, dm), BF16)] * n_mix,
        compiler_params=_cparams(("parallel", "parallel")),
        name="rwkv_pre",
    )(h, h, h, g.reshape(1, dm), mu)


def _lora_kernel(x_ref, w1_ref, o_ref, *, act):
    hid = _dot(x_ref[...], w1_ref[0])
    if act == "tanh":
        hid = jnp.tanh(hid)
    elif act == "sigmoid":
        hid = _sigmoid(hid)
    o_ref[0] = hid.astype(o_ref.dtype)


def _pad_rank(w1, w2):
    rpad = -w1.shape[-1] % LANES
    return (jnp.pad(w1, ((0, 0), (0, 0), (0, rpad))).astype(BF16),
            jnp.pad(w2, ((0, 0), (0, rpad), (0, 0))).astype(BF16))


def lora_hidden(x, w1, *, act, tm=1024):
    m, dm = x.shape
    nd, _, rp = w1.shape
    tm = min(tm, m)
    return pl.pallas_call(
        functools.partial(_lora_kernel, act=act),
        grid=(nd, m // tm),
        in_specs=[pl.BlockSpec((tm, dm), lambda d, i: (i, 0)),
                  pl.BlockSpec((1, dm, rp), lambda d, i: (d, 0, 0))],
        out_specs=pl.BlockSpec((1, tm, rp), lambda d, i: (d, i, 0)),
        out_shape=jax.ShapeDtypeStruct((nd, m, rp), BF16),
        compiler_params=_cparams(("parallel", "parallel")),
        name="lora_" + act,
    )(x, w1)


def _rwkv_post_kernel(y_ref, b_ref, hg_ref, g2_ref, lw_ref, lb_ref, o_ref):
    dm = o_ref.shape[-1]
    seg = (lax.broadcasted_iota(jnp.int32, (GW, GW), 0) // HEAD
           == lax.broadcasted_iota(jnp.int32, (GW, GW), 1) // HEAD).astype(BF16)
    hg = hg_ref[0]
    for s in range(dm // GW):
        sl = slice(s * GW, (s + 1) * GW)
        y = y_ref[0, :, sl].astype(F32) + y_ref[1, :, sl].astype(F32)
        mean = _dot(y, seg) * (1.0 / HEAD)
        yc = y - mean
        var = _dot(yc * yc, seg) * (1.0 / HEAD)
        yn = yc * lax.rsqrt(var + GN_EPS) * lw_ref[:, sl] + lb_ref[:, sl]
        out = yn + (b_ref[0, :, sl].astype(F32) + b_ref[1, :, sl].astype(F32))
        gate = _dot(hg, g2_ref[:, sl])
        o_ref[:, sl] = (out * gate).astype(o_ref.dtype)


def rwkv_post(y, bonus, hg, g2, ln_w, ln_b, *, tm=512):
    _, m, dm = y.shape
    rank = hg.shape[-1]
    tm = min(tm, m)
    yspec = pl.BlockSpec((2, tm, dm), lambda i: (0, i, 0))
    pspec = pl.BlockSpec((1, dm), lambda i: (0, 0))
    return pl.pallas_call(
        _rwkv_post_kernel,
        grid=(m // tm,),
        in_specs=[yspec, yspec, pl.BlockSpec((1, tm, rank), lambda i: (0, i, 0)),
                  pl.BlockSpec((rank, dm), lambda i: (0, 0)), pspec, pspec],
        out_specs=pl.BlockSpec((tm, dm), lambda i: (i, 0)),
        out_shape=jax.ShapeDtypeStruct((m, dm), BF16),
        compiler_params=_cparams(("parallel",)),
        name="rwkv_post",
    )(y, bonus, hg, g2, ln_w.reshape(1, dm), ln_b.reshape(1, dm))


def _ffn_kernel(be_ref, nu_ref, x_ref, wg_ref, wu_ref, wd_ref, *rest, has_res):
    del be_ref
    o_ref = rest[-1]
    f = pl.program_id(1)

    @pl.when(f == 0)
    def _():
        o_ref[...] = rest[0][...] if has_res else jnp.zeros_like(o_ref)

    @pl.when(pl.program_id(0) < nu_ref[0])
    def _():
        x = x_ref[...].astype(BF16)
        gate = _dot(x, wg_ref[0])
        up = _dot(x, wu_ref[0])
        o_ref[...] += _dot(gate * _sigmoid(gate) * up, wd_ref[0])


def ffn(x, blk_e, n_used, wg, wu, wd, *, res=None, tm, tf):
    p, dm = x.shape
    fdim = wg.shape[-1]
    assert p % tm == 0 and fdim % tf == 0
    nf = fdim // tf

    def blk(i, nu):
        return jnp.minimum(i, nu[0] - 1)

    def fblk(i, f, nu):
        return jnp.where(i < nu[0], f, nf - 1)

    row_spec = pl.BlockSpec((tm, dm), lambda i, f, be, nu: (i, 0))
    in_specs = [pl.BlockSpec((tm, dm), lambda i, f, be, nu: (blk(i, nu), 0)),
                pl.BlockSpec((1, dm, tf), lambda i, f, be, nu: (be[blk(i, nu)], 0, fblk(i, f, nu))),
                pl.BlockSpec((1, dm, tf), lambda i, f, be, nu: (be[blk(i, nu)], 0, fblk(i, f, nu))),
                pl.BlockSpec((1, tf, dm), lambda i, f, be, nu: (be[blk(i, nu)], fblk(i, f, nu), 0))]
    args = [x, wg, wu, wd]
    if res is not None:
        in_specs.append(row_spec)
        args.append(res)
    return pl.pallas_call(
        functools.partial(_ffn_kernel, has_res=res is not None),
        grid_spec=pltpu.PrefetchScalarGridSpec(
            num_scalar_prefetch=2, grid=(p // tm, nf),
            in_specs=in_specs, out_specs=row_spec),
        out_shape=jax.ShapeDtypeStruct((p, dm), F32),
        compiler_params=_cparams(("parallel", "arbitrary")),
        name="ffn",
    )(blk_e, n_used, *args)


def _rope_table_kernel(pos_ref, inv_ref, cos_ref, sin_ref):
    ang = pos_ref[...].astype(F32) * inv_ref[...]
    lane = lax.broadcasted_iota(jnp.int32, ang.shape, 1)
    cos_ref[...] = jnp.cos(ang)
    sin_ref[...] = jnp.where((lane % HEAD) < HEAD // 2, -jnp.sin(ang), jnp.sin(ang))


def rope_table(positions, *, tm=1024):
    m = positions.shape[0]
    tm = min(tm, m)
    half = HEAD // 2
    inv = ROPE_THETA ** (-(2.0 * (jnp.arange(LANES) % half)).astype(F32) / HEAD)
    spec = pl.BlockSpec((tm, LANES), lambda i: (i, 0))
    return pl.pallas_call(
        _rope_table_kernel,
        grid=(m // tm,),
        in_specs=[pl.BlockSpec((tm, 1), lambda i: (i, 0)), pl.BlockSpec((1, LANES), lambda i: (0, 0))],
        out_specs=[spec, spec],
        out_shape=[jax.ShapeDtypeStruct((m, LANES), F32)] * 2,
        compiler_params=_cparams(("parallel",)),
        name="rope_table",
    )(positions.reshape(m, 1), inv.reshape(1, LANES))


def _proj_rope_kernel(a_ref, b_ref, cos_ref, sin_ref, o_ref, *, scale):
    acc = _dot(a_ref[...], b_ref[...])
    cos = cos_ref[...]
    sin_signed = sin_ref[...]
    lane = lax.broadcasted_iota(jnp.int32, cos.shape, 1)
    first = (lane % HEAD) < HEAD // 2
    for s in range(acc.shape[1] // LANES):
        sl = slice(s * LANES, (s + 1) * LANES)
        t = acc[:, sl]
        partner = jnp.where(first, pltpu.roll(t, LANES - HEAD // 2, 1), pltpu.roll(t, HEAD // 2, 1))
        o_ref[:, sl] = ((t * cos + partner * sin_signed) * scale).astype(o_ref.dtype)


def _proj_ones_kernel(a_ref, b_ref, o_ref):
    acc = _dot(a_ref[...], b_ref[...])
    lane = lax.broadcasted_iota(jnp.int32, (acc.shape[0], LANES), 1)
    ones_col = jnp.where(lane == 0, 1.0, 0.0).astype(o_ref.dtype)
    for s in range(acc.shape[1] // LANES):
        o_ref[:, 2 * s * LANES:(2 * s + 1) * LANES] = acc[:, s * LANES:(s + 1) * LANES].astype(o_ref.dtype)
        o_ref[:, (2 * s + 1) * LANES:(2 * s + 2) * LANES] = ones_col


def qkv_rope(hn, w_qkv, cos, sin_signed, *, tm=1024, tn=512):
    m, dm = hn.shape
    assert 2 * HEAD == LANES
    tm, tn = min(tm, m), min(tn, dm)
    nb = dm // tn
    a_spec = pl.BlockSpec((tm, dm), lambda i, j: (i, 0))
    t_spec = pl.BlockSpec((tm, LANES), lambda i, j: (i, 0))

    def b_spec(part):
        return pl.BlockSpec((dm, tn), lambda i, j: (0, j + part * nb))

    def rope_proj(part, scale):
        return pl.pallas_call(
            functools.partial(_proj_rope_kernel, scale=scale),
            grid=(m // tm, nb),
            in_specs=[a_spec, b_spec(part), t_spec, t_spec],
            out_specs=pl.BlockSpec((tm, tn), lambda i, j: (i, j)),
            out_shape=jax.ShapeDtypeStruct((m, dm), BF16),
            compiler_params=_cparams(("parallel", "parallel")),
            name="proj_rope",
        )(hn, w_qkv, cos, sin_signed)

    q = rope_proj(0, HEAD ** -0.5 * LOG2E)
    k = rope_proj(1, 1.0)
    v1 = pl.pallas_call(
        _proj_ones_kernel,
        grid=(m // tm, nb),
        in_specs=[a_spec, b_spec(2)],
        out_specs=pl.BlockSpec((tm, 2 * tn), lambda i, j: (i, j)),
        out_shape=jax.ShapeDtypeStruct((m, 2 * dm), BF16),
        compiler_params=_cparams(("parallel", "parallel")),
        name="proj_ones",
    )(hn, w_qkv)
    return q, k, v1


def _attn_kernel(q_ref, k_ref, v_ref, lq1_ref, lk1_ref, lq2_ref, lk2_ref, sub_ref, o_ref,
                 m_ref, acc_ref, s_ref, *, lambda_init, tk):
    tq, hw = q_ref.shape
    seq = k_ref.shape[0]
    m_ref[...] = jnp.full_like(m_ref, -jnp.inf)
    acc_ref[...] = jnp.zeros_like(acc_ref)
    q = q_ref[...]
    lane = lax.broadcasted_iota(jnp.int32, q.shape, 1)
    qs = [jnp.where((lane >= c * HEAD) & (lane < (c + 1) * HEAD), q, jnp.zeros((), q.dtype)) for c in range(2)]

    def rows(j):
        return pl.ds(pl.multiple_of(j * tk, tk), tk)

    def scores(j, slot):
        ks = k_ref[rows(j), :]
        for c in range(2):
            s_ref[slot, c] = _dot_nt(qs[c], ks)

    def accumulate(j, slot):
        vs = v_ref[rows(j), :]
        for c in range(2):
            s = s_ref[slot, c]
            m_prev = m_ref[c]
            m_new = jnp.maximum(m_prev, jnp.max(s, axis=1, keepdims=True))
            alpha = jnp.exp2(m_prev - m_new)
            p = jnp.exp2(s - jnp.concatenate([m_new] * (tk // hw), axis=1))
            acc_ref[c] = acc_ref[c] * jnp.concatenate([alpha, alpha], axis=1) + _dot(p, vs)
            m_ref[c] = m_new

    n_chunks = seq // tk

    def body(jj, carry):
        scores(2 * jj + 1, 1)
        accumulate(2 * jj, 0)
        scores(jnp.minimum(2 * jj + 2, n_chunks - 1), 0)
        accumulate(2 * jj + 1, 1)
        return carry

    scores(0, 0)
    lax.fori_loop(0, n_chunks // 2, body, 0)
    lam = (jnp.exp(jnp.sum(lq1_ref[...] * lk1_ref[...], keepdims=True))
           - jnp.exp(jnp.sum(lq2_ref[...] * lk2_ref[...], keepdims=True)) + lambda_init)
    a1 = acc_ref[0]
    a2 = acc_ref[1]
    o = a1[:, :hw] / a1[:, hw:hw + 1] - lam * (a2[:, :hw] / a2[:, hw:hw + 1])
    o = o * lax.rsqrt(jnp.mean(o * o, axis=-1, keepdims=True) + SUBLN_EPS) * sub_ref[...] * (1.0 - lambda_init)
    o_ref[...] = o.astype(o_ref.dtype)


def diff_attn(q, k, v1, lq1, lk1, lq2, lk2, subln, lambda_init, *, bsz, tq=512, tk=512):
    m, dm = q.shape
    seq = m // bsz
    hw = 2 * HEAD
    tq, tk = min(tq, seq), min(tk, seq // 2)
    assert seq % tq == 0 and seq % (2 * tk) == 0
    nq = seq // tq
    q_spec = pl.BlockSpec((tq, hw), lambda b, h, i: (b * nq + i, h))
    l_spec = pl.BlockSpec((1, HEAD), lambda b, h, i: (0, 0))
    return pl.pallas_call(
        functools.partial(_attn_kernel, lambda_init=lambda_init, tk=tk),
        grid=(bsz, dm // hw, nq),
        in_specs=[q_spec, pl.BlockSpec((seq, hw), lambda b, h, i: (b, h)),
                  pl.BlockSpec((seq, 2 * hw), lambda b, h, i: (b, h)), l_spec, l_spec, l_spec, l_spec,
                  pl.BlockSpec((1, hw), lambda b, h, i: (0, 0))],
        out_specs=q_spec,
        out_shape=jax.ShapeDtypeStruct((m, dm), BF16),
        scratch_shapes=[pltpu.VMEM((2, tq, hw), F32), pltpu.VMEM((2, tq, 2 * hw), F32),
                        pltpu.VMEM((2, 2, tq, tk), F32)],
        compiler_params=_cparams(("parallel", "parallel", "parallel")),
        name="diff_attn",
    )(q, k, v1, lq1.reshape(1, HEAD), lk1.reshape(1, HEAD), lq2.reshape(1, HEAD), lk2.reshape(1, HEAD),
      subln.reshape(1, hw))


def _router_kernel(h_ref, g_ref, wr_ref, hn_ref, e_ref, gate_ref, rank_ref, cnt_ref, *, n_exp):
    tm = h_ref.shape[0]

    @pl.when(pl.program_id(0) == 0)
    def _():
        cnt_ref[...] = jnp.zeros_like(cnt_ref)

    hn = _rms(h_ref[...], g_ref[...])
    hn_ref[...] = hn
    lane = lax.broadcasted_iota(jnp.int32, (tm, LANES), 1)
    logits = jnp.where(lane < n_exp, _dot_f32(hn, wr_ref[...]), -jnp.inf)
    m1 = jnp.max(logits, axis=1, keepdims=True)
    i1 = jnp.min(jnp.where(logits == m1, lane, LANES), axis=1, keepdims=True)
    rest = jnp.where(lane == i1, -jnp.inf, logits)
    m2 = jnp.max(rest, axis=1, keepdims=True)
    i2 = jnp.min(jnp.where(rest == m2, lane, LANES), axis=1, keepdims=True)
    e21 = jnp.exp(m2 - m1)
    g1 = 1.0 / (1.0 + e21)
    g2 = e21 / (1.0 + e21)
    oh1 = lane == i1
    oh2 = lane == i2
    both = jnp.where(oh1 | oh2, 1.0, 0.0)
    earlier = (lax.broadcasted_iota(jnp.int32, (tm, tm), 0) > lax.broadcasted_iota(jnp.int32, (tm, tm), 1))
    cum = _dot(earlier.astype(BF16), both) + cnt_ref[...]
    r1 = jnp.sum(jnp.where(oh1, cum, 0.0), axis=1, keepdims=True)
    r2 = jnp.sum(jnp.where(oh2, cum, 0.0), axis=1, keepdims=True)
    cnt_ref[...] += jnp.sum(both, axis=0, keepdims=True)
    e_ref[...] = jnp.where(lane == 0, i1, jnp.where(lane == 1, i2, 0))
    gate_ref[...] = jnp.where(lane == 0, g1, jnp.where(lane == 1, g2, 0.0))
    rank_ref[...] = jnp.where(lane == 0, r1, jnp.where(lane == 1, r2, 0.0)).astype(jnp.int32)


def moe_route(h, g, w_router, *, tm=512):
    m, dm = h.shape
    n_exp = w_router.shape[1]
    tm = min(tm, m)
    wr = jnp.pad(w_router, ((0, 0), (0, LANES - n_exp)))
    row = lambda dt: jax.ShapeDtypeStruct((m, LANES), dt)
    lspec = pl.BlockSpec((tm, LANES), lambda i: (i, 0))
    return pl.pallas_call(
        functools.partial(_router_kernel, n_exp=n_exp),
        grid=(m // tm,),
        in_specs=[pl.BlockSpec((tm, dm), lambda i: (i, 0)), pl.BlockSpec((1, dm), lambda i: (0, 0)),
                  pl.BlockSpec((dm, LANES), lambda i: (0, 0))],
        out_specs=[pl.BlockSpec((tm, dm), lambda i: (i, 0)), lspec, lspec, lspec,
                   pl.BlockSpec((1, LANES), lambda i: (0, 0))],
        out_shape=[jax.ShapeDtypeStruct((m, dm), F32), row(jnp.int32), row(F32), row(jnp.int32),
                   jax.ShapeDtypeStruct((1, LANES), F32)],
        compiler_params=_cparams(("arbitrary",)),
        name="moe_route",
    )(h, g.reshape(1, dm), wr)


def _dest_kernel(e_ref, rank_ref, cnt_ref, dest_ref, be_ref, *, n_exp, bm):
    tm = e_ref.shape[0]
    nb = be_ref.shape[0]
    cnt = cnt_ref[...]
    pcnt = jnp.floor((cnt + (bm - 1)) * (1.0 / bm)) * bm
    upper = (lax.broadcasted_iota(jnp.int32, (LANES, LANES), 0)
             <= lax.broadcasted_iota(jnp.int32, (LANES, LANES), 1)).astype(F32)
    pends = _dot_f32(jnp.broadcast_to(pcnt, (8, LANES)), upper)[0:1]
    pstart = pends - pcnt
    lane = lax.broadcasted_iota(jnp.int32, (tm, LANES), 1)
    e = e_ref[...]
    d1 = jnp.sum(jnp.where(lane == e[:, 0:1], pstart, 0.0), axis=1, keepdims=True)
    d2 = jnp.sum(jnp.where(lane == e[:, 1:2], pstart, 0.0), axis=1, keepdims=True)
    dest = jnp.where(lane == 0, d1, jnp.where(lane == 1, d2, 0.0)).astype(jnp.int32) + rank_ref[...]
    dest_ref[...] = dest
    blk_lane = lax.broadcasted_iota(jnp.int32, (nb, LANES), 1)
    blk_row = lax.broadcasted_iota(jnp.int32, (nb, LANES), 0).astype(F32) * bm
    ended = jnp.where((blk_lane < n_exp) & (pends <= blk_row), 1.0, 0.0)
    be = jnp.minimum(jnp.sum(ended, axis=1, keepdims=True), n_exp - 1.0)
    n_used = jnp.sum(jnp.where(blk_lane == n_exp - 1, pends, 0.0), axis=1, keepdims=True) * (1.0 / bm)
    be_ref[...] = jnp.where(blk_lane == 1, n_used, be).astype(jnp.int32)


def moe_dest(e, rank, cnt, *, n_exp, bm, nblk, tm=512):
    m = e.shape[0]
    tm = min(tm, m)
    nb = -(-nblk // 8) * 8
    lspec = pl.BlockSpec((tm, LANES), lambda i: (i, 0))
    return pl.pallas_call(
        functools.partial(_dest_kernel, n_exp=n_exp, bm=bm),
        grid=(m // tm,),
        in_specs=[lspec, lspec, pl.BlockSpec((1, LANES), lambda i: (0, 0))],
        out_specs=[lspec, pl.BlockSpec((nb, LANES), lambda i: (0, 0))],
        out_shape=[jax.ShapeDtypeStruct((m, LANES), jnp.int32), jax.ShapeDtypeStruct((nb, LANES), jnp.int32)],
        compiler_params=_cparams(("arbitrary",)),
        name="moe_dest",
    )(e, rank, cnt)


def _row_copy(src, dst, sem):
    return pltpu.make_async_copy(src, dst, sem)


def _dispatch_kernel(dest_ref, hn_ref, xs_in_ref, xs_ref, sem):
    del xs_in_ref
    tm = hn_ref.shape[0]
    base = pl.program_id(0) * tm

    def issue(r, carry):
        for s in range(TOP_K):
            dst = dest_ref[TOP_K * (base + r) + s]
            _row_copy(hn_ref.at[pl.ds(r, 1)], xs_ref.at[pl.ds(dst, 1)], sem).start()
        return carry

    lax.fori_loop(0, tm, issue, 0)

    def drain(r, carry):
        for s in range(TOP_K):
            _row_copy(hn_ref.at[pl.ds(0, 1)], xs_ref.at[pl.ds(0, 1)], sem).wait()
        return carry

    lax.fori_loop(0, tm, drain, 0)


def moe_dispatch(dest_flat, hn, n_rows, *, tm=256):
    m, dm = hn.shape
    tm = min(tm, m)
    xs0 = jnp.zeros((n_rows, dm), hn.dtype)
    return pl.pallas_call(
        _dispatch_kernel,
        grid_spec=pltpu.PrefetchScalarGridSpec(
            num_scalar_prefetch=1, grid=(m // tm,),
            in_specs=[pl.BlockSpec((tm, dm), lambda i, d: (i, 0)), pl.BlockSpec(memory_space=pl.ANY)],
            out_specs=pl.BlockSpec(memory_space=pl.ANY),
            scratch_shapes=[pltpu.SemaphoreType.DMA(())]),
        out_shape=jax.ShapeDtypeStruct((n_rows, dm), hn.dtype),
        input_output_aliases={2: 0},
        compiler_params=_cparams(("arbitrary",)),
        name="moe_dispatch",
    )(dest_flat, hn, xs0)


def _combine_kernel(dest_ref, yb_ref, gate_ref, res_ref, *rest, has_norm):
    norm_ref = rest[0] if has_norm else None
    o_ref, buf_ref, sem = rest[-3:]
    tm = res_ref.shape[0]
    base = pl.program_id(0) * tm

    def issue(r, carry):
        for s in range(TOP_K):
            src = dest_ref[TOP_K * (base + r) + s]
            _row_copy(yb_ref.at[pl.ds(src, 1)], buf_ref.at[s, pl.ds(r, 1)], sem).start()
        return carry

    lax.fori_loop(0, tm, issue, 0)

    def drain(r, carry):
        for s in range(TOP_K):
            _row_copy(yb_ref.at[pl.ds(0, 1)], buf_ref.at[s, pl.ds(0, 1)], sem).wait()
        return carry

    lax.fori_loop(0, tm, drain, 0)
    gate = gate_ref[...]
    out = res_ref[...] + (gate[:, 0:1] * buf_ref[0] + gate[:, 1:2] * buf_ref[1])
    o_ref[...] = _rms(out, norm_ref[...]) if norm_ref is not None else out


def moe_combine(dest_flat, yb, gates, res, *, norm_g=None, tm=256):
    m, dm = res.shape
    tm = min(tm, m)
    row_spec = pl.BlockSpec((tm, dm), lambda i, d: (i, 0))
    in_specs = [pl.BlockSpec(memory_space=pl.ANY), pl.BlockSpec((tm, LANES), lambda i, d: (i, 0)), row_spec]
    args = [dest_flat, yb, gates, res]
    if norm_g is not None:
        in_specs.append(pl.BlockSpec((1, dm), lambda i, d: (0, 0)))
        args.append(norm_g.reshape(1, dm))
    return pl.pallas_call(
        functools.partial(_combine_kernel, has_norm=norm_g is not None),
        grid_spec=pltpu.PrefetchScalarGridSpec(
            num_scalar_prefetch=1, grid=(m // tm,),
            in_specs=in_specs,
            out_specs=row_spec,
            scratch_shapes=[pltpu.VMEM((TOP_K, tm, dm), F32), pltpu.SemaphoreType.DMA(())]),
        out_shape=jax.ShapeDtypeStruct((m, dm), F32),
        compiler_params=_cparams(("arbitrary",)),
        name="moe_combine",
    )(*args)


def moe_layer(h, g, w_router, wg, wu, wd, *, norm_g=None):
    m, dm = h.shape
    n_exp = w_router.shape[1]
    bm = 1024 if m * TOP_K >= 8 * 1024 else 128
    n_rows = m * TOP_K + n_exp * bm
    nblk = n_rows // bm
    hn, e, gates, rank, cnt = moe_route(h, g, w_router)
    dest, be = moe_dest(e, rank, cnt, n_exp=n_exp, bm=bm, nblk=nblk)
    dest_flat = dest[:, :TOP_K].reshape(-1)
    xs = moe_dispatch(dest_flat, hn, n_rows)
    tf = 256 if wg.shape[-1] % 256 == 0 else LANES
    yb = ffn(xs, be[:nblk, 0], be[0, 1:2], wg, wu, wd, tm=bm, tf=tf)
    return moe_combine(dest_flat, yb, gates, h, norm_g=norm_g)


def rwkv_layer(h, bsz, g_norm, mu, w_rkv, g1, g2, w0, w1, w2, a0, a1, a2, k_k, k_a, r_k, ln_w, ln_b, w_o):
    m, dm = h.shape
    seq = m // bsz
    xr, xk, xv, xw, xa, xg = (t.reshape(m, dm) for t in rwkv_pre(h.reshape(bsz, seq, dm), g_norm, mu))
    w_rkv = w_rkv.astype(BF16)
    r, k, v = (matmul(xi, w_rkv, n=dm, col_off=n * dm, out_dtype=BF16).reshape(bsz, seq, dm)
               for n, xi in enumerate((xr, xk, xv)))
    g1, g2 = _pad_rank(g1[None], g2[None])
    w1, w2 = _pad_rank(w1, w2)
    a1, a2 = _pad_rank(a1, a2)
    hg = lora_hidden(xg, g1, act="sigmoid")
    hw = lora_hidden(xw, w1, act="tanh").reshape(2, bsz, seq, -1)
    ha = lora_hidden(xa, a1, act="none").reshape(2, bsz, seq, -1)
    y, bonus = wkv_bidir(r, k, v, hw, ha, w2, a2, w0, a0, k_k, k_a, r_k)
    z = rwkv_post(y.reshape(2, m, dm), bonus.reshape(2, m, dm), hg, g2[0], ln_w, ln_b)
    return matmul(z, w_o.astype(BF16), res=h)


def attn_layer(h, bsz, positions, g_norm, w_qkv, lq1, lk1, lq2, lk2, subln, w_o, lambda_init):
    hn = rmsnorm(h, g_norm, out_dtype=BF16)
    cos, sin_signed = rope_table(positions.reshape(-1))
    q, k, v1 = qkv_rope(hn, w_qkv.astype(BF16), cos, sin_signed)
    o = diff_attn(q, k, v1, lq1, lk1, lq2, lk2, subln, lambda_init, bsz=bsz)
    return matmul(o, w_o.astype(BF16), res=h)


def dense_ffn_layer(h, g_norm, wg, wu, wd):
    m = h.shape[0]
    hn = rmsnorm(h, g_norm, out_dtype=BF16)
    tm = min(512, m)
    tf = 1024 if wg.shape[-1] % 1024 == 0 else LANES
    return ffn(hn, jnp.zeros((m // tm,), jnp.int32), jnp.full((1,), m // tm, jnp.int32),
               wg.astype(BF16)[None], wu.astype(BF16)[None], wd.astype(BF16)[None], res=h, tm=tm, tf=tf)


def kernel(x, positions, norm_mix, norm_ffn, norm_final, rw_mu, rw_w_rkv, rw_g1, rw_g2, rw_w0, rw_w1, rw_w2, rw_a0, rw_a1, rw_a2, rw_kk, rw_ka, rw_rk, rw_ln_w, rw_ln_b, rw_w_o, da_w_qkv, da_lq1, da_lk1, da_lq2, da_lk2, da_subln, da_w_o, ff_wg, ff_wu, ff_wd, moe_router, moe_wg, moe_wu, moe_wd):
    bsz, seq, dm = x.shape
    depth = norm_mix.shape[0]
    h = x.reshape(bsz * seq, dm)
    for i in range(depth):
        j = i // 2
        if i % 2 == 0:
            h = rwkv_layer(h, bsz, norm_mix[i], rw_mu[j], rw_w_rkv[j], rw_g1[j], rw_g2[j], rw_w0[j], rw_w1[j],
                           rw_w2[j], rw_a0[j], rw_a1[j], rw_a2[j], rw_kk[j], rw_ka[j], rw_rk[j],
                           rw_ln_w[j], rw_ln_b[j], rw_w_o[j])
            h = dense_ffn_layer(h, norm_ffn[i], ff_wg[j], ff_wu[j], ff_wd[j])
        else:
            lambda_init = 0.8 - 0.6 * math.exp(-0.3 * i)
            h = attn_layer(h, bsz, positions, norm_mix[i], da_w_qkv[j], da_lq1[j], da_lk1[j], da_lq2[j],
                           da_lk2[j], da_subln[j], da_w_o[j], lambda_init)
            last = i == depth - 1
            h = moe_layer(h, norm_ffn[i], moe_router[j], moe_wg[j], moe_wu[j], moe_wd[j],
                          norm_g=norm_final if last else None)
    if depth % 2 == 1 or depth == 0:
        h = rmsnorm(h, norm_final, out_dtype=x.dtype)
    return h.reshape(bsz, seq, dm)
```

```python
import functools
import math

import jax
import jax.numpy as jnp
from jax import lax
from jax.experimental import pallas as pl
from jax.experimental.pallas import tpu as pltpu

F32 = jnp.float32
BF16 = jnp.bfloat16

RMS_EPS = 1e-6
HEAD = 64
GN_EPS = HEAD * 1e-5
ROPE_THETA = 10000.0
SUBLN_EPS = 1e-5
LOG2E = math.log2(math.e)
TOP_K = 2
LANES = 128
CHUNK = 64
GROUP = 4
ROW_SPLIT = 4
GW = GROUP * HEAD
VMEM_LIMIT = 56 * 1024 * 1024


def _cparams(sem, vmem=VMEM_LIMIT):
    return pltpu.CompilerParams(dimension_semantics=sem, vmem_limit_bytes=vmem)


def _dot(a, b):
    return jnp.dot(a.astype(BF16), b.astype(BF16), preferred_element_type=F32)


def _dot_f32(a, b):
    return jnp.dot(a, b, preferred_element_type=F32, precision=lax.Precision.HIGHEST)


def _split2(x):
    hi = x.astype(BF16)
    return hi, (x - hi.astype(F32)).astype(BF16)


def _dot_nt(a, b):
    return lax.dot_general(a.astype(BF16), b.astype(BF16), (((1,), (1,)), ((), ())),
                           preferred_element_type=F32)


def _dot_tn(a, b):
    return lax.dot_general(a.astype(BF16), b.astype(BF16), (((0,), (0,)), ((), ())),
                           preferred_element_type=F32)


def _mm_kernel(a_ref, b_ref, *rest, has_res, has_norm):
    acc = _dot(a_ref[...], b_ref[...])
    pos = 0
    if has_res:
        acc = acc + rest[pos][...]
        pos += 1
    if has_norm:
        g_ref, o_ref, hn_ref = rest[pos:]
        hn_ref[...] = _rms(acc, g_ref[...]).astype(hn_ref.dtype)
    else:
        o_ref = rest[pos]
    o_ref[...] = acc.astype(o_ref.dtype)


def matmul(a, b, *, n=None, col_off=0, res=None, norm_g=None, out_dtype=F32, tm=1024, tn=1024):
    m, k = a.shape
    n = b.shape[1] if n is None else n
    if norm_g is not None:
        tm, tn = min(tm, 512), n
    tm, tn = min(tm, m), min(tn, n)
    assert m % tm == 0 and n % tn == 0 and col_off % tn == 0
    off = col_off // tn
    o_spec = pl.BlockSpec((tm, tn), lambda i, j: (i, j))
    in_specs = [pl.BlockSpec((tm, k), lambda i, j: (i, 0)),
                pl.BlockSpec((k, tn), lambda i, j: (0, j + off))]
    args = [a, b]
    if res is not None:
        in_specs.append(o_spec)
        args.append(res)
    out_specs, out_shape = o_spec, jax.ShapeDtypeStruct((m, n), out_dtype)
    if norm_g is not None:
        in_specs.append(pl.BlockSpec((1, n), lambda i, j: (0, 0)))
        args.append(norm_g.reshape(1, n))
        out_specs, out_shape = [o_spec, o_spec], [out_shape, jax.ShapeDtypeStruct((m, n), BF16)]
    return pl.pallas_call(
        functools.partial(_mm_kernel, has_res=res is not None, has_norm=norm_g is not None),
        grid=(m // tm, n // tn),
        in_specs=in_specs,
        out_specs=out_specs,
        out_shape=out_shape,
        compiler_params=_cparams(("parallel", "parallel")),
        name="matmul",
    )(*args)


def _wkv_kernel(r_ref, k_ref, v_ref, hw_ref, ha_ref, w2_ref, a2_ref, w0_ref, a0_ref, kk_ref, ka_ref, rk_ref,
                y_ref, bonus_ref, state_ref, *, n_groups):
    L = CHUNK
    d = pl.program_id(0)
    sgn = 1 - 2 * d

    @pl.when(pl.program_id(3) == 0)
    def _():
        state_ref[...] = jnp.zeros_like(state_ref)

    row = lax.broadcasted_iota(jnp.int32, (L, GW), 0)
    col = lax.broadcasted_iota(jnp.int32, (L, GW), 1) % L
    diff = (row - col) * sgn
    strict = diff > 0
    incl = diff >= 0
    eye = diff == 0
    level_masks = []
    s = 1
    while s < L:
        level_masks.append(strict & ((row // (2 * s)) == (col // (2 * s))) & ((row // s) != (col // s)))
        s *= 2
    bd_mask = (lax.broadcasted_iota(jnp.int32, (GROUP * L, GW), 0) // L
               == lax.broadcasted_iota(jnp.int32, (GROUP * L, GW), 1) // HEAD)
    seg_ones = bd_mask.astype(BF16)
    tri = (lax.broadcasted_iota(jnp.int32, (L, L), 0) - lax.broadcasted_iota(jnp.int32, (L, L), 1)) * sgn >= 0
    tri = tri.astype(F32)

    heads_per_tile = LANES // HEAD
    tile_lane_head = lax.broadcasted_iota(jnp.int32, (L, LANES), 1) // HEAD
    zero_tile = jnp.zeros((L, LANES), BF16)

    def bd(x):
        xb = x.astype(BF16)
        blocks = []
        for j in range(GROUP):
            tile = j // heads_per_tile
            piece = jnp.where(tile_lane_head == j % heads_per_tile,
                              xb[:, tile * LANES:(tile + 1) * LANES], jnp.zeros((), BF16))
            blocks.append(jnp.concatenate([piece if t == tile else zero_tile for t in range(GW // LANES)], axis=1))
        return jnp.concatenate(blocks, axis=0)

    def mm(x, yb):
        return jnp.dot(x.astype(BF16), yb, preferred_element_type=F32)

    def bdmm(x, y):
        return mm(x, bd(y))

    def fold(full):
        fm = jnp.where(bd_mask, full, 0.0)
        out = fm[0:HEAD]
        for j in range(1, GROUP):
            out = out + fm[j * HEAD:(j + 1) * HEAD]
        return out

    groups = range(n_groups)
    sls = [slice(g * GW, (g + 1) * GW) for g in groups]
    r = [r_ref[0, :, sl].astype(F32) for sl in sls]
    k = [k_ref[0, :, sl].astype(F32) for sl in sls]
    v = [v_ref[0, :, sl].astype(F32) for sl in sls]

    wz_all = _dot(hw_ref[0, 0], w2_ref[0]) + w0_ref[0]
    az_all = _dot(ha_ref[0, 0], a2_ref[0]) + a0_ref[0]
    lw, a_s = [], []
    for sl in sls:
        u = -wz_all[:, sl]
        softplus = jnp.maximum(u, 0.0) + jnp.log(1.0 + jnp.exp(-jnp.abs(u)))
        lw.append(-jnp.exp(-softplus - 0.5))
        a_s.append(1.0 / (1.0 + jnp.exp(-az_all[:, sl])))
    split = [_split2(x) for x in lw]
    c = [_dot(tri, hi) + _dot(tri, lo) for hi, lo in split]
    kkr = [k[g] * kk_ref[:, sls[g]] for g in groups]
    ss = [_dot(x * x, seg_ones) for x in kkr]
    kk = [kkr[g] / jnp.maximum(jnp.sqrt(ss[g]), 1e-12) for g in groups]
    kd = [k[g] * (1.0 + (a_s[g] - 1.0) * ka_ref[:, sls[g]]) for g in groups]
    b_vec = [kk[g] * a_s[g] for g in groups]
    rkd = [_dot(r[g] * kd[g] * rk_ref[:, sls[g]], seg_ones) for g in groups]
    bonus_ref[0, 0] = jnp.concatenate([rkd[g] * v[g] for g in groups], axis=1).astype(bonus_ref.dtype)

    tot = [jnp.sum(x, axis=0, keepdims=True) for x in lw]
    e_nc = [jnp.exp(-x) for x in c]
    e_tc = [jnp.exp(tot[g] - c[g]) for g in groups]
    at = [-kk[g] * jnp.exp(c[g] - lw[g]) for g in groups]
    rt = [r[g] * jnp.exp(c[g]) for g in groups]
    bt = [b_vec[g] * e_nc[g] for g in groups]
    kt = [kd[g] * e_nc[g] for g in groups]
    bh = [b_vec[g] * e_tc[g] for g in groups]
    kh = [kd[g] * e_tc[g] for g in groups]

    ar = [jnp.concatenate([at[g], rt[g]], axis=0) for g in groups]
    pb = [_dot_nt(ar[g], bd(bt[g])) for g in groups]
    pk = [_dot_nt(ar[g], bd(kt[g])) for g in groups]
    a_ab = [jnp.where(strict, x[:L], 0.0) for x in pb]
    a_rb = [jnp.where(incl, x[L:], 0.0) for x in pb]
    a_ak = [jnp.where(strict, x[:L], 0.0) for x in pk]
    a_rk = [jnp.where(incl, x[L:], 0.0) for x in pk]
    v_bd = [bd(x) for x in v]
    x_loc = [mm(a_ak[g], v_bd[g]) for g in groups]

    t_inv = [jnp.where(eye, 1.0, 0.0) + jnp.where(level_masks[0], a, 0.0) for a in a_ab]
    for m in level_masks[1:]:
        t1 = [bdmm(t_inv[g], jnp.where(m, a_ab[g], 0.0)) for g in groups]
        t_inv = [t_inv[g] + bdmm(t1[g], t_inv[g]) for g in groups]

    u_loc = [bdmm(t_inv[g], x_loc[g]) for g in groups]
    ta = [bdmm(t_inv[g], at[g]) for g in groups]
    gb = [fold(_dot_tn(bh[g], ta[g])) for g in groups]
    hm = [fold(_dot_tn(jnp.concatenate([bh[g], kh[g]], axis=0), jnp.concatenate([u_loc[g], v[g]], axis=0)))
          for g in groups]
    q = [rt[g] + bdmm(a_rb[g], ta[g]) for g in groups]
    y_loc = [bdmm(a_rb[g], u_loc[g]) + mm(a_rk[g], v_bd[g]) for g in groups]

    m0 = [state_ref[g] for g in groups]
    m0_bd = [bd(x) for x in m0]
    y_ref[0, 0] = jnp.concatenate([mm(q[g], m0_bd[g]) + y_loc[g] for g in groups], axis=1).astype(y_ref.dtype)
    for g in groups:
        e_diag = jnp.where(eye, jnp.broadcast_to(jnp.exp(tot[g]), (L, GW)), 0.0)
        ed_hi, ed_lo = _split2(e_diag)
        w_sbs = _dot(ed_hi, seg_ones) + _dot(ed_lo, seg_ones)
        state_ref[g] = w_sbs * m0[g] + mm(gb[g], m0_bd[g]) + hm[g]


def wkv_bidir(r, k, v, hw, ha, w2, a2, w0, a0, k_k, k_a, r_k, *, n_groups=8):
    bsz, seq, dm = r.shape
    rank = hw.shape[-1]
    assert CHUNK == HEAD and seq % CHUNK == 0
    n_groups = min(n_groups, dm // GW)
    w = n_groups * GW
    assert dm % w == 0
    nc = seq // CHUNK

    def cidx(d, c):
        return c + d * (nc - 1 - 2 * c)

    x_spec = pl.BlockSpec((1, CHUNK, w), lambda d, b, j, c: (b, cidx(d, c), j))
    d_spec = pl.BlockSpec((1, 1, CHUNK, w), lambda d, b, j, c: (d, b, cidx(d, c), j))
    p_spec = pl.BlockSpec((1, w), lambda d, b, j, c: (0, j))
    h_spec = pl.BlockSpec((1, 1, CHUNK, rank), lambda d, b, j, c: (d, b, cidx(d, c), 0))
    w2_spec = pl.BlockSpec((1, rank, w), lambda d, b, j, c: (d, 0, j))
    b_spec = pl.BlockSpec((1, 1, w), lambda d, b, j, c: (d, 0, j))
    out_sd = jax.ShapeDtypeStruct((2, bsz, seq, dm), BF16)
    return pl.pallas_call(
        functools.partial(_wkv_kernel, n_groups=n_groups),
        grid=(2, bsz, dm // w, nc),
        in_specs=[x_spec, x_spec, x_spec, h_spec, h_spec, w2_spec, w2_spec, b_spec, b_spec,
                  p_spec, p_spec, p_spec],
        out_specs=[d_spec, d_spec],
        out_shape=[out_sd, out_sd],
        scratch_shapes=[pltpu.VMEM((n_groups, HEAD, GW), F32)],
        compiler_params=_cparams(("parallel", "parallel", "parallel", "arbitrary")),
        name="wkv_bidir",
    )(r, k, v, hw, ha, w2, a2, w0.reshape(2, 1, dm), a0.reshape(2, 1, dm),
      k_k.reshape(1, dm), k_a.reshape(1, dm), r_k.reshape(1, dm))


def _rms(x, g):
    return x * lax.rsqrt(jnp.mean(x * x, axis=-1, keepdims=True) + RMS_EPS) * g


def _sigmoid(x):
    return 1.0 / (1.0 + jnp.exp(-x))


def _rmsnorm_kernel(x_ref, g_ref, o_ref):
    o_ref[...] = _rms(x_ref[...], g_ref[...]).astype(o_ref.dtype)


def rmsnorm(x, g, *, out_dtype, tm=512):
    m, dm = x.shape
    tm = min(tm, m)
    return pl.pallas_call(
        _rmsnorm_kernel,
        grid=(m // tm,),
        in_specs=[pl.BlockSpec((tm, dm), lambda i: (i, 0)), pl.BlockSpec((1, dm), lambda i: (0, 0))],
        out_specs=pl.BlockSpec((tm, dm), lambda i: (i, 0)),
        out_shape=jax.ShapeDtypeStruct((m, dm), out_dtype),
        compiler_params=_cparams(("parallel",)),
        name="rmsnorm",
    )(x, g.reshape(1, dm))


def _rwkv_pre_kernel(x_ref, xp_ref, xn_ref, g_ref, mu_ref, *o_refs, ts):
    i = pl.program_id(1)
    g = g_ref[...]
    hn = _rms(x_ref[0], g)
    h_before = jnp.where(i == 0, 0.0, _rms(xp_ref[0], g)[7:8])
    h_after = jnp.where(i == pl.num_programs(1) - 1, 0.0, _rms(xn_ref[0], g)[0:1])
    row = lax.broadcasted_iota(jnp.int32, hn.shape, 0)
    prev = jnp.where(row == 0, h_before, pltpu.roll(hn, 1, 0))
    nxt = jnp.where(row == ts - 1, h_after, pltpu.roll(hn, ts - 1, 0))
    delta = 0.5 * (prev + nxt) - hn
    for n, o_ref in enumerate(o_refs):
        o_ref[0] = (hn + delta * mu_ref[n:n + 1, :]).astype(o_ref.dtype)


def rwkv_pre(h, g, mu, *, ts=256):
    bsz, seq, dm = h.shape
    ts = min(ts, seq)
    n_mix = mu.shape[0]
    sub = 8
    nsub = ts // sub
    x_spec = pl.BlockSpec((1, ts, dm), lambda b, i: (b, i, 0))
    return pl.pallas_call(
        functools.partial(_rwkv_pre_kernel, ts=ts),
        grid=(bsz, seq // ts),
        in_specs=[x_spec,
                  pl.BlockSpec((1, sub, dm), lambda b, i: (b, jnp.maximum(i * nsub - 1, 0), 0)),
                  pl.BlockSpec((1, sub, dm), lambda b, i: (b, jnp.minimum((i + 1) * nsub, seq // sub - 1), 0)),
                  pl.BlockSpec((1, dm), lambda b, i: (0, 0)),
                  pl.BlockSpec((n_mix, dm), lambda b, i: (0, 0))],
        out_specs=[x_spec] * n_mix,
        out_shape=[jax.ShapeDtypeStruct((bsz, seq, dm), BF16)] * n_mix,
        compiler_params=_cparams(("parallel", "parallel")),
        name="rwkv_pre",
    )(h, h, h, g.reshape(1, dm), mu)


def _lora_kernel(x_ref, w1_ref, o_ref, *, act):
    hid = _dot(x_ref[...], w1_ref[0])
    if act == "tanh":
        hid = jnp.tanh(hid)
    elif act == "sigmoid":
        hid = _sigmoid(hid)
    o_ref[0] = hid.astype(o_ref.dtype)


def _pad_rank(w1, w2):
    rpad = -w1.shape[-1] % LANES
    return (jnp.pad(w1, ((0, 0), (0, 0), (0, rpad))).astype(BF16),
            jnp.pad(w2, ((0, 0), (0, rpad), (0, 0))).astype(BF16))


def lora_hidden(x, w1, *, act, tm=1024):
    m, dm = x.shape
    nd, _, rp = w1.shape
    tm = min(tm, m)
    return pl.pallas_call(
        functools.partial(_lora_kernel, act=act),
        grid=(nd, m // tm),
        in_specs=[pl.BlockSpec((tm, dm), lambda d, i: (i, 0)),
                  pl.BlockSpec((1, dm, rp), lambda d, i: (d, 0, 0))],
        out_specs=pl.BlockSpec((1, tm, rp), lambda d, i: (d, i, 0)),
        out_shape=jax.ShapeDtypeStruct((nd, m, rp), BF16),
        compiler_params=_cparams(("parallel", "parallel")),
        name="lora_" + act,
    )(x, w1)


def _rwkv_post_kernel(y_ref, b_ref, hg_ref, g2_ref, lw_ref, lb_ref, o_ref):
    dm = o_ref.shape[-1]
    seg = (lax.broadcasted_iota(jnp.int32, (GW, GW), 0) // HEAD
           == lax.broadcasted_iota(jnp.int32, (GW, GW), 1) // HEAD).astype(BF16)
    hg = hg_ref[0]
    for s in range(dm // GW):
        sl = slice(s * GW, (s + 1) * GW)
        y = y_ref[0, :, sl].astype(F32) + y_ref[1, :, sl].astype(F32)
        mean = _dot(y, seg) * (1.0 / HEAD)
        yc = y - mean
        var = _dot(yc * yc, seg) * (1.0 / HEAD)
        yn = yc * lax.rsqrt(var + GN_EPS) * lw_ref[:, sl] + lb_ref[:, sl]
        out = yn + (b_ref[0, :, sl].astype(F32) + b_ref[1, :, sl].astype(F32))
        gate = _dot(hg, g2_ref[:, sl])
        o_ref[:, sl] = (out * gate).astype(o_ref.dtype)


def rwkv_post(y, bonus, hg, g2, ln_w, ln_b, *, tm=512):
    _, m, dm = y.shape
    rank = hg.shape[-1]
    tm = min(tm, m)
    yspec = pl.BlockSpec((2, tm, dm), lambda i: (0, i, 0))
    pspec = pl.BlockSpec((1, dm), lambda i: (0, 0))
    return pl.pallas_call(
        _rwkv_post_kernel,
        grid=(m // tm,),
        in_specs=[yspec, yspec, pl.BlockSpec((1, tm, rank), lambda i: (0, i, 0)),
                  pl.BlockSpec((rank, dm), lambda i: (0, 0)), pspec, pspec],
        out_specs=pl.BlockSpec((tm, dm), lambda i: (i, 0)),
        out_shape=jax.ShapeDtypeStruct((m, dm), BF16),
        compiler_params=_cparams(("parallel",)),
        name="rwkv_post",
    )(y, bonus, hg, g2, ln_w.reshape(1, dm), ln_b.reshape(1, dm))


def _ffn_kernel(be_ref, nu_ref, nq_ref, x_ref, wg_ref, wu_ref, wd_ref, *rest, has_res, has_norm):
    del be_ref, nu_ref
    tm = x_ref.shape[0]
    res_ref = rest[0] if has_res else None
    g_ref = rest[int(has_res)] if has_norm else None
    o_ref = rest[-2] if has_norm else rest[-1]
    f = pl.program_id(1)

    @pl.when(f == 0)
    def _():
        o_ref[...] = res_ref[...] if has_res else jnp.zeros_like(o_ref)

    n_quarters = nq_ref[pl.program_id(0)]
    for q in range(1, ROW_SPLIT + 1):
        rows = q * tm // ROW_SPLIT

        @pl.when(n_quarters == q)
        def _(rows=rows):
            x = x_ref[:rows].astype(BF16)
            gate = _dot(x, wg_ref[0])
            up = _dot(x, wu_ref[0])
            o_ref[:rows] += _dot(gate * _sigmoid(gate) * up, wd_ref[0])

    if has_norm:
        @pl.when(f == pl.num_programs(1) - 1)
        def _():
            rest[-1][...] = _rms(o_ref[...], g_ref[...]).astype(rest[-1].dtype)


def ffn(x, blk_e, n_used, n_quarters, wg, wu, wd, *, res=None, norm_g=None, tm, tf):
    p, dm = x.shape
    fdim = wg.shape[-1]
    assert p % tm == 0 and fdim % tf == 0 and tm % (ROW_SPLIT * 16) == 0
    nf = fdim // tf

    def blk(i, nu):
        return jnp.minimum(i, nu[0] - 1)

    def fblk(i, f, nu):
        return jnp.where(i < nu[0], f, nf - 1)

    row_spec = pl.BlockSpec((tm, dm), lambda i, f, be, nu, nq: (i, 0))
    in_specs = [pl.BlockSpec((tm, dm), lambda i, f, be, nu, nq: (blk(i, nu), 0)),
                pl.BlockSpec((1, dm, tf), lambda i, f, be, nu, nq: (be[blk(i, nu)], 0, fblk(i, f, nu))),
                pl.BlockSpec((1, dm, tf), lambda i, f, be, nu, nq: (be[blk(i, nu)], 0, fblk(i, f, nu))),
                pl.BlockSpec((1, tf, dm), lambda i, f, be, nu, nq: (be[blk(i, nu)], fblk(i, f, nu), 0))]
    args = [x, wg, wu, wd]
    if res is not None:
        in_specs.append(row_spec)
        args.append(res)
    out_specs, out_shape = row_spec, jax.ShapeDtypeStruct((p, dm), F32)
    if norm_g is not None:
        in_specs.append(pl.BlockSpec((1, dm), lambda i, f, be, nu, nq: (0, 0)))
        args.append(norm_g.reshape(1, dm))
        out_specs, out_shape = [row_spec, row_spec], [out_shape, jax.ShapeDtypeStruct((p, dm), BF16)]
    return pl.pallas_call(
        functools.partial(_ffn_kernel, has_res=res is not None, has_norm=norm_g is not None),
        grid_spec=pltpu.PrefetchScalarGridSpec(
            num_scalar_prefetch=3, grid=(p // tm, nf),
            in_specs=in_specs, out_specs=out_specs),
        out_shape=out_shape,
        compiler_params=_cparams(("parallel", "arbitrary")),
        name="ffn",
    )(blk_e, n_used, n_quarters, *args)


def _rope_table_kernel(pos_ref, inv_ref, cos_ref, sin_ref):
    ang = pos_ref[...].astype(F32) * inv_ref[...]
    lane = lax.broadcasted_iota(jnp.int32, ang.shape, 1)
    cos_ref[...] = jnp.cos(ang)
    sin_ref[...] = jnp.where((lane % HEAD) < HEAD // 2, -jnp.sin(ang), jnp.sin(ang))


def rope_table(positions, *, tm=1024):
    m = positions.shape[0]
    tm = min(tm, m)
    half = HEAD // 2
    inv = ROPE_THETA ** (-(2.0 * (jnp.arange(LANES) % half)).astype(F32) / HEAD)
    spec = pl.BlockSpec((tm, LANES), lambda i: (i, 0))
    return pl.pallas_call(
        _rope_table_kernel,
        grid=(m // tm,),
        in_specs=[pl.BlockSpec((tm, 1), lambda i: (i, 0)), pl.BlockSpec((1, LANES), lambda i: (0, 0))],
        out_specs=[spec, spec],
        out_shape=[jax.ShapeDtypeStruct((m, LANES), F32)] * 2,
        compiler_params=_cparams(("parallel",)),
        name="rope_table",
    )(positions.reshape(m, 1), inv.reshape(1, LANES))


def _proj_rope_kernel(a_ref, b_ref, cos_ref, sin_ref, o_ref, *, scale):
    acc = _dot(a_ref[...], b_ref[...])
    cos = cos_ref[...]
    sin_signed = sin_ref[...]
    lane = lax.broadcasted_iota(jnp.int32, cos.shape, 1)
    first = (lane % HEAD) < HEAD // 2
    for s in range(acc.shape[1] // LANES):
        sl = slice(s * LANES, (s + 1) * LANES)
        t = acc[:, sl]
        partner = jnp.where(first, pltpu.roll(t, LANES - HEAD // 2, 1), pltpu.roll(t, HEAD // 2, 1))
        o_ref[:, sl] = ((t * cos + partner * sin_signed) * scale).astype(o_ref.dtype)


def _proj_ones_kernel(a_ref, b_ref, o_ref):
    acc = _dot(a_ref[...], b_ref[...])
    lane = lax.broadcasted_iota(jnp.int32, (acc.shape[0], LANES), 1)
    ones_col = jnp.where(lane == 0, 1.0, 0.0).astype(o_ref.dtype)
    for s in range(acc.shape[1] // LANES):
        o_ref[:, 2 * s * LANES:(2 * s + 1) * LANES] = acc[:, s * LANES:(s + 1) * LANES].astype(o_ref.dtype)
        o_ref[:, (2 * s + 1) * LANES:(2 * s + 2) * LANES] = ones_col


def qkv_rope(hn, w_qkv, cos, sin_signed, *, tm=1024, tn=512):
    m, dm = hn.shape
    assert 2 * HEAD == LANES
    tm, tn = min(tm, m), min(tn, dm)
    nb = dm // tn
    a_spec = pl.BlockSpec((tm, dm), lambda i, j: (i, 0))
    t_spec = pl.BlockSpec((tm, LANES), lambda i, j: (i, 0))

    def b_spec(part):
        return pl.BlockSpec((dm, tn), lambda i, j: (0, j + part * nb))

    def rope_proj(part, scale):
        return pl.pallas_call(
            functools.partial(_proj_rope_kernel, scale=scale),
            grid=(m // tm, nb),
            in_specs=[a_spec, b_spec(part), t_spec, t_spec],
            out_specs=pl.BlockSpec((tm, tn), lambda i, j: (i, j)),
            out_shape=jax.ShapeDtypeStruct((m, dm), BF16),
            compiler_params=_cparams(("parallel", "parallel")),
            name="proj_rope",
        )(hn, w_qkv, cos, sin_signed)

    q = rope_proj(0, HEAD ** -0.5 * LOG2E)
    k = rope_proj(1, 1.0)
    v1 = pl.pallas_call(
        _proj_ones_kernel,
        grid=(m // tm, nb),
        in_specs=[a_spec, b_spec(2)],
        out_specs=pl.BlockSpec((tm, 2 * tn), lambda i, j: (i, j)),
        out_shape=jax.ShapeDtypeStruct((m, 2 * dm), BF16),
        compiler_params=_cparams(("parallel", "parallel")),
        name="proj_ones",
    )(hn, w_qkv)
    return q, k, v1


def _attn_kernel(q_ref, k_ref, v_ref, lq1_ref, lk1_ref, lq2_ref, lk2_ref, sub_ref, o_ref,
                 m_ref, acc_ref, s_ref, *, lambda_init, tk):
    tq, hw = q_ref.shape
    seq = k_ref.shape[0]
    m_ref[...] = jnp.full_like(m_ref, -jnp.inf)
    acc_ref[...] = jnp.zeros_like(acc_ref)
    q = q_ref[...]
    lane = lax.broadcasted_iota(jnp.int32, q.shape, 1)
    qs = [jnp.where((lane >= c * HEAD) & (lane < (c + 1) * HEAD), q, jnp.zeros((), q.dtype)) for c in range(2)]

    def rows(j):
        return pl.ds(pl.multiple_of(j * tk, tk), tk)

    def scores(j, slot):
        ks = k_ref[rows(j), :]
        for c in range(2):
            s_ref[slot, c] = _dot_nt(qs[c], ks)

    def accumulate(j, slot):
        vs = v_ref[rows(j), :]
        for c in range(2):
            s = s_ref[slot, c]
            m_prev = m_ref[c]
            m_new = jnp.maximum(m_prev, jnp.max(s, axis=1, keepdims=True))
            alpha = jnp.exp2(m_prev - m_new)
            p = jnp.exp2(s - jnp.concatenate([m_new] * (tk // hw), axis=1))
            acc_ref[c] = acc_ref[c] * jnp.concatenate([alpha, alpha], axis=1) + _dot(p, vs)
            m_ref[c] = m_new

    n_chunks = seq // tk

    def body(jj, carry):
        scores(2 * jj + 1, 1)
        accumulate(2 * jj, 0)
        scores(2 * jj + 2, 0)
        accumulate(2 * jj + 1, 1)
        return carry

    scores(0, 0)
    lax.fori_loop(0, n_chunks // 2 - 1, body, 0)
    scores(n_chunks - 1, 1)
    accumulate(n_chunks - 2, 0)
    accumulate(n_chunks - 1, 1)
    lam = (jnp.exp(jnp.sum(lq1_ref[...] * lk1_ref[...], keepdims=True))
           - jnp.exp(jnp.sum(lq2_ref[...] * lk2_ref[...], keepdims=True)) + lambda_init)
    a1 = acc_ref[0]
    a2 = acc_ref[1]
    o = a1[:, :hw] / a1[:, hw:hw + 1] - lam * (a2[:, :hw] / a2[:, hw:hw + 1])
    o = o * lax.rsqrt(jnp.mean(o * o, axis=-1, keepdims=True) + SUBLN_EPS) * sub_ref[...] * (1.0 - lambda_init)
    o_ref[...] = o.astype(o_ref.dtype)


def diff_attn(q, k, v1, lq1, lk1, lq2, lk2, subln, lambda_init, *, bsz, tq=512, tk=2048):
    m, dm = q.shape
    seq = m // bsz
    hw = 2 * HEAD
    tq, tk = min(tq, seq), min(tk, seq // 2)
    assert seq % tq == 0 and seq % (2 * tk) == 0
    nq = seq // tq
    q_spec = pl.BlockSpec((tq, hw), lambda b, h, i: (b * nq + i, h))
    l_spec = pl.BlockSpec((1, HEAD), lambda b, h, i: (0, 0))
    return pl.pallas_call(
        functools.partial(_attn_kernel, lambda_init=lambda_init, tk=tk),
        grid=(bsz, dm // hw, nq),
        in_specs=[q_spec, pl.BlockSpec((seq, hw), lambda b, h, i: (b, h)),
                  pl.BlockSpec((seq, 2 * hw), lambda b, h, i: (b, h)), l_spec, l_spec, l_spec, l_spec,
                  pl.BlockSpec((1, hw), lambda b, h, i: (0, 0))],
        out_specs=q_spec,
        out_shape=jax.ShapeDtypeStruct((m, dm), BF16),
        scratch_shapes=[pltpu.VMEM((2, tq, hw), F32), pltpu.VMEM((2, tq, 2 * hw), F32),
                        pltpu.VMEM((2, 2, tq, tk), F32)],
        compiler_params=_cparams(("parallel", "parallel", "parallel")),
        name="diff_attn",
    )(q, k, v1, lq1.reshape(1, HEAD), lk1.reshape(1, HEAD), lq2.reshape(1, HEAD), lk2.reshape(1, HEAD),
      subln.reshape(1, hw))


def _router_kernel(h_ref, g_ref, wr_ref, hn_ref, e_ref, gate_ref, rank_ref, cnt_ref, *, n_exp):
    tm = h_ref.shape[0]

    @pl.when(pl.program_id(0) == 0)
    def _():
        cnt_ref[...] = jnp.zeros_like(cnt_ref)

    hn = _rms(h_ref[...], g_ref[...])
    hn_ref[...] = hn
    lane = lax.broadcasted_iota(jnp.int32, (tm, LANES), 1)
    logits = jnp.where(lane < n_exp, _dot_f32(hn, wr_ref[...]), -jnp.inf)
    m1 = jnp.max(logits, axis=1, keepdims=True)
    i1 = jnp.min(jnp.where(logits == m1, lane, LANES), axis=1, keepdims=True)
    rest = jnp.where(lane == i1, -jnp.inf, logits)
    m2 = jnp.max(rest, axis=1, keepdims=True)
    i2 = jnp.min(jnp.where(rest == m2, lane, LANES), axis=1, keepdims=True)
    e21 = jnp.exp(m2 - m1)
    g1 = 1.0 / (1.0 + e21)
    g2 = e21 / (1.0 + e21)
    oh1 = lane == i1
    oh2 = lane == i2
    both = jnp.where(oh1 | oh2, 1.0, 0.0)
    earlier = (lax.broadcasted_iota(jnp.int32, (tm, tm), 0) > lax.broadcasted_iota(jnp.int32, (tm, tm), 1))
    cum = _dot(earlier.astype(BF16), both) + cnt_ref[...]
    r1 = jnp.sum(jnp.where(oh1, cum, 0.0), axis=1, keepdims=True)
    r2 = jnp.sum(jnp.where(oh2, cum, 0.0), axis=1, keepdims=True)
    cnt_ref[...] += jnp.sum(both, axis=0, keepdims=True)
    e_ref[...] = jnp.where(lane == 0, i1, jnp.where(lane == 1, i2, 0))
    gate_ref[...] = jnp.where(lane == 0, g1, jnp.where(lane == 1, g2, 0.0))
    rank_ref[...] = jnp.where(lane == 0, r1, jnp.where(lane == 1, r2, 0.0)).astype(jnp.int32)


def moe_route(h, g, w_router, *, tm=512):
    m, dm = h.shape
    n_exp = w_router.shape[1]
    tm = min(tm, m)
    wr = jnp.pad(w_router, ((0, 0), (0, LANES - n_exp)))
    row = lambda dt: jax.ShapeDtypeStruct((m, LANES), dt)
    lspec = pl.BlockSpec((tm, LANES), lambda i: (i, 0))
    return pl.pallas_call(
        functools.partial(_router_kernel, n_exp=n_exp),
        grid=(m // tm,),
        in_specs=[pl.BlockSpec((tm, dm), lambda i: (i, 0)), pl.BlockSpec((1, dm), lambda i: (0, 0)),
                  pl.BlockSpec((dm, LANES), lambda i: (0, 0))],
        out_specs=[pl.BlockSpec((tm, dm), lambda i: (i, 0)), lspec, lspec, lspec,
                   pl.BlockSpec((1, LANES), lambda i: (0, 0))],
        out_shape=[jax.ShapeDtypeStruct((m, dm), F32), row(jnp.int32), row(F32), row(jnp.int32),
                   jax.ShapeDtypeStruct((1, LANES), F32)],
        compiler_params=_cparams(("arbitrary",)),
        name="moe_route",
    )(h, g.reshape(1, dm), wr)


def _dest_kernel(e_ref, rank_ref, cnt_ref, dest_ref, be_ref, *, n_exp, bm):
    tm = e_ref.shape[0]
    nb = be_ref.shape[0]
    cnt = cnt_ref[...]
    pcnt = jnp.floor((cnt + (bm - 1)) * (1.0 / bm)) * bm
    upper = (lax.broadcasted_iota(jnp.int32, (LANES, LANES), 0)
             <= lax.broadcasted_iota(jnp.int32, (LANES, LANES), 1)).astype(F32)
    pends = _dot_f32(jnp.broadcast_to(pcnt, (8, LANES)), upper)[0:1]
    pstart = pends - pcnt
    lane = lax.broadcasted_iota(jnp.int32, (tm, LANES), 1)
    e = e_ref[...]
    d1 = jnp.sum(jnp.where(lane == e[:, 0:1], pstart, 0.0), axis=1, keepdims=True)
    d2 = jnp.sum(jnp.where(lane == e[:, 1:2], pstart, 0.0), axis=1, keepdims=True)
    dest = jnp.where(lane == 0, d1, jnp.where(lane == 1, d2, 0.0)).astype(jnp.int32) + rank_ref[...]
    dest_ref[...] = dest
    blk_lane = lax.broadcasted_iota(jnp.int32, (nb, LANES), 1)
    blk_row = lax.broadcasted_iota(jnp.int32, (nb, LANES), 0).astype(F32) * bm
    ended = jnp.where((blk_lane < n_exp) & (pends <= blk_row), 1.0, 0.0)
    be = jnp.minimum(jnp.sum(ended, axis=1, keepdims=True), n_exp - 1.0)
    n_used = jnp.sum(jnp.where(blk_lane == n_exp - 1, pends, 0.0), axis=1, keepdims=True) * (1.0 / bm)
    seg_end = jnp.sum(jnp.where(blk_lane.astype(F32) == be, pstart + cnt, 0.0), axis=1, keepdims=True)
    valid = jnp.clip(seg_end - blk_row[:, 0:1], 0.0, float(bm))
    quarters = jnp.ceil(valid * (ROW_SPLIT / bm))
    be_ref[...] = jnp.where(blk_lane == 1, n_used, jnp.where(blk_lane == 2, quarters, be)).astype(jnp.int32)


def moe_dest(e, rank, cnt, *, n_exp, bm, nblk, tm=512):
    m = e.shape[0]
    tm = min(tm, m)
    nb = -(-nblk // 8) * 8
    lspec = pl.BlockSpec((tm, LANES), lambda i: (i, 0))
    return pl.pallas_call(
        functools.partial(_dest_kernel, n_exp=n_exp, bm=bm),
        grid=(m // tm,),
        in_specs=[lspec, lspec, pl.BlockSpec((1, LANES), lambda i: (0, 0))],
        out_specs=[lspec, pl.BlockSpec((nb, LANES), lambda i: (0, 0))],
        out_shape=[jax.ShapeDtypeStruct((m, LANES), jnp.int32), jax.ShapeDtypeStruct((nb, LANES), jnp.int32)],
        compiler_params=_cparams(("arbitrary",)),
        name="moe_dest",
    )(e, rank, cnt)


def _row_copy(src, dst, sem):
    return pltpu.make_async_copy(src, dst, sem)


def _dispatch_kernel(dest_ref, hn_ref, xs_in_ref, xs_ref, sem):
    del xs_in_ref
    tm = hn_ref.shape[0]
    base = pl.program_id(0) * tm

    def issue(r, carry):
        for s in range(TOP_K):
            dst = dest_ref[TOP_K * (base + r) + s]
            _row_copy(hn_ref.at[pl.ds(r, 1)], xs_ref.at[pl.ds(dst, 1)], sem).start()
        return carry

    lax.fori_loop(0, tm, issue, 0)

    def drain(r, carry):
        for s in range(TOP_K):
            _row_copy(hn_ref.at[pl.ds(0, 1)], xs_ref.at[pl.ds(0, 1)], sem).wait()
        return carry

    lax.fori_loop(0, tm, drain, 0)


def moe_dispatch(dest_flat, hn, n_rows, *, tm=256):
    m, dm = hn.shape
    tm = min(tm, m)
    xs0 = jnp.zeros((n_rows, dm), hn.dtype)
    return pl.pallas_call(
        _dispatch_kernel,
        grid_spec=pltpu.PrefetchScalarGridSpec(
            num_scalar_prefetch=1, grid=(m // tm,),
            in_specs=[pl.BlockSpec((tm, dm), lambda i, d: (i, 0)), pl.BlockSpec(memory_space=pl.ANY)],
            out_specs=pl.BlockSpec(memory_space=pl.ANY),
            scratch_shapes=[pltpu.SemaphoreType.DMA(())]),
        out_shape=jax.ShapeDtypeStruct((n_rows, dm), hn.dtype),
        input_output_aliases={2: 0},
        compiler_params=_cparams(("arbitrary",)),
        name="moe_dispatch",
    )(dest_flat, hn, xs0)


def _combine_kernel(dest_ref, yb_ref, gate_ref, res_ref, *rest, has_norm):
    norm_ref = rest[0] if has_norm else None
    o_ref, buf_ref, sem = rest[-3:]
    tm = res_ref.shape[0]
    base = pl.program_id(0) * tm

    def issue(r, carry):
        for s in range(TOP_K):
            src = dest_ref[TOP_K * (base + r) + s]
            _row_copy(yb_ref.at[pl.ds(src, 1)], buf_ref.at[s, pl.ds(r, 1)], sem).start()
        return carry

    lax.fori_loop(0, tm, issue, 0)

    def drain(r, carry):
        for s in range(TOP_K):
            _row_copy(yb_ref.at[pl.ds(0, 1)], buf_ref.at[s, pl.ds(0, 1)], sem).wait()
        return carry

    lax.fori_loop(0, tm, drain, 0)
    gate = gate_ref[...]
    out = res_ref[...] + (gate[:, 0:1] * buf_ref[0] + gate[:, 1:2] * buf_ref[1])
    o_ref[...] = _rms(out, norm_ref[...]) if norm_ref is not None else out


def moe_combine(dest_flat, yb, gates, res, *, norm_g=None, tm=256):
    m, dm = res.shape
    tm = min(tm, m)
    row_spec = pl.BlockSpec((tm, dm), lambda i, d: (i, 0))
    in_specs = [pl.BlockSpec(memory_space=pl.ANY), pl.BlockSpec((tm, LANES), lambda i, d: (i, 0)), row_spec]
    args = [dest_flat, yb, gates, res]
    if norm_g is not None:
        in_specs.append(pl.BlockSpec((1, dm), lambda i, d: (0, 0)))
        args.append(norm_g.reshape(1, dm))
    return pl.pallas_call(
        functools.partial(_combine_kernel, has_norm=norm_g is not None),
        grid_spec=pltpu.PrefetchScalarGridSpec(
            num_scalar_prefetch=1, grid=(m // tm,),
            in_specs=in_specs,
            out_specs=row_spec,
            scratch_shapes=[pltpu.VMEM((TOP_K, tm, dm), F32), pltpu.SemaphoreType.DMA(())]),
        out_shape=jax.ShapeDtypeStruct((m, dm), F32),
        compiler_params=_cparams(("arbitrary",)),
        name="moe_combine",
    )(*args)


def moe_layer(h, g, w_router, wg, wu, wd, *, norm_g=None):
    m, dm = h.shape
    n_exp = w_router.shape[1]
    bm = 1024 if m * TOP_K >= 8 * 1024 else 128
    n_rows = m * TOP_K + n_exp * bm
    nblk = n_rows // bm
    hn, e, gates, rank, cnt = moe_route(h, g, w_router)
    dest, be = moe_dest(e, rank, cnt, n_exp=n_exp, bm=bm, nblk=nblk)
    dest_flat = dest[:, :TOP_K].reshape(-1)
    xs = moe_dispatch(dest_flat, hn, n_rows)
    tf = 256 if wg.shape[-1] % 256 == 0 else LANES
    yb = ffn(xs, be[:nblk, 0], be[0, 1:2], be[:nblk, 2], wg, wu, wd, tm=bm, tf=tf)
    return moe_combine(dest_flat, yb, gates, h, norm_g=norm_g)


def rwkv_layer(h, bsz, g_norm, mu, w_rkv, g1, g2, w0, w1, w2, a0, a1, a2, k_k, k_a, r_k, ln_w, ln_b, w_o,
               g_next):
    m, dm = h.shape
    seq = m // bsz
    xr, xk, xv, xw, xa, xg = (t.reshape(m, dm) for t in rwkv_pre(h.reshape(bsz, seq, dm), g_norm, mu))
    w_rkv = w_rkv.astype(BF16)
    r, k, v = (matmul(xi, w_rkv, n=dm, col_off=n * dm, out_dtype=BF16).reshape(bsz, seq, dm)
               for n, xi in enumerate((xr, xk, xv)))
    g1, g2 = _pad_rank(g1[None], g2[None])
    w1, w2 = _pad_rank(w1, w2)
    a1, a2 = _pad_rank(a1, a2)
    hg = lora_hidden(xg, g1, act="sigmoid")
    hw = lora_hidden(xw, w1, act="tanh").reshape(2, bsz, seq, -1)
    ha = lora_hidden(xa, a1, act="none").reshape(2, bsz, seq, -1)
    y, bonus = wkv_bidir(r, k, v, hw, ha, w2, a2, w0, a0, k_k, k_a, r_k)
    z = rwkv_post(y.reshape(2, m, dm), bonus.reshape(2, m, dm), hg, g2[0], ln_w, ln_b)
    return matmul(z, w_o.astype(BF16), res=h, norm_g=g_next)


def attn_layer(h, hn, bsz, positions, w_qkv, lq1, lk1, lq2, lk2, subln, w_o, lambda_init):
    cos, sin_signed = rope_table(positions.reshape(-1))
    q, k, v1 = qkv_rope(hn, w_qkv.astype(BF16), cos, sin_signed)
    o = diff_attn(q, k, v1, lq1, lk1, lq2, lk2, subln, lambda_init, bsz=bsz)
    return matmul(o, w_o.astype(BF16), res=h)


def dense_ffn_layer(h, hn, wg, wu, wd, g_next):
    m = h.shape[0]
    tm = min(512, m)
    tf = 1024 if wg.shape[-1] % 1024 == 0 else LANES
    nblk = m // tm
    return ffn(hn, jnp.zeros((nblk,), jnp.int32), jnp.full((1,), nblk, jnp.int32),
               jnp.full((nblk,), ROW_SPLIT, jnp.int32), wg.astype(BF16)[None], wu.astype(BF16)[None],
               wd.astype(BF16)[None], res=h, norm_g=g_next, tm=tm, tf=tf)


def kernel(x, positions, norm_mix, norm_ffn, norm_final, rw_mu, rw_w_rkv, rw_g1, rw_g2, rw_w0, rw_w1, rw_w2, rw_a0, rw_a1, rw_a2, rw_kk, rw_ka, rw_rk, rw_ln_w, rw_ln_b, rw_w_o, da_w_qkv, da_lq1, da_lk1, da_lq2, da_lk2, da_subln, da_w_o, ff_wg, ff_wu, ff_wd, moe_router, moe_wg, moe_wu, moe_wd):
    bsz, seq, dm = x.shape
    depth = norm_mix.shape[0]
    h = x.reshape(bsz * seq, dm)
    hn = None
    for i in range(depth):
        j = i // 2
        if i % 2 == 0:
            h, hn = rwkv_layer(h, bsz, norm_mix[i], rw_mu[j], rw_w_rkv[j], rw_g1[j], rw_g2[j], rw_w0[j], rw_w1[j],
                               rw_w2[j], rw_a0[j], rw_a1[j], rw_a2[j], rw_kk[j], rw_ka[j], rw_rk[j],
                               rw_ln_w[j], rw_ln_b[j], rw_w_o[j], norm_ffn[i])
            if i + 1 < depth:
                h, hn = dense_ffn_layer(h, hn, ff_wg[j], ff_wu[j], ff_wd[j], norm_mix[i + 1])
            else:
                h = dense_ffn_layer(h, hn, ff_wg[j], ff_wu[j], ff_wd[j], None)
        else:
            lambda_init = 0.8 - 0.6 * math.exp(-0.3 * i)
            h = attn_layer(h, hn, bsz, positions, da_w_qkv[j], da_lq1[j], da_lk1[j], da_lq2[j],
                           da_lk2[j], da_subln[j], da_w_o[j], lambda_init)
            last = i == depth - 1
            h = moe_layer(h, norm_ffn[i], moe_router[j], moe_wg[j], moe_wu[j], moe_wd[j],
                          norm_g=norm_final if last else None)
    if depth % 2 == 1 or depth == 0:
        h = rmsnorm(h, norm_final, out_dtype=x.dtype)
    return h.reshape(bsz, seq, dm)
```

```python
import functools
import math

import jax
import jax.numpy as jnp
from jax import lax
from jax.experimental import pallas as pl
from jax.experimental.pallas import tpu as pltpu

F32 = jnp.float32
BF16 = jnp.bfloat16

RMS_EPS = 1e-6
HEAD = 64
GN_EPS = HEAD * 1e-5
ROPE_THETA = 10000.0
SUBLN_EPS = 1e-5
LOG2E = math.log2(math.e)
TOP_K = 2
LANES = 128
CHUNK = 64
GROUP = 2
WKV_STEP_CHUNKS = 2
ROW_SPLIT = 4
GW = GROUP * HEAD
VMEM_LIMIT = 56 * 1024 * 1024


def _cparams(sem, vmem=VMEM_LIMIT):
    return pltpu.CompilerParams(dimension_semantics=sem, vmem_limit_bytes=vmem)


def _dot(a, b):
    return jnp.dot(a.astype(BF16), b.astype(BF16), preferred_element_type=F32)


def _dot_f32(a, b):
    return jnp.dot(a, b, preferred_element_type=F32, precision=lax.Precision.HIGHEST)


def _split2(x):
    hi = x.astype(BF16)
    return hi, (x - hi.astype(F32)).astype(BF16)


def _dot_nt(a, b):
    return lax.dot_general(a.astype(BF16), b.astype(BF16), (((1,), (1,)), ((), ())),
                           preferred_element_type=F32)


def _dot_tn(a, b):
    return lax.dot_general(a.astype(BF16), b.astype(BF16), (((0,), (0,)), ((), ())),
                           preferred_element_type=F32)


def _mm_kernel(a_ref, b_ref, *rest, has_res, has_norm):
    acc = _dot(a_ref[...], b_ref[...])
    pos = 0
    if has_res:
        acc = acc + rest[pos][...]
        pos += 1
    if has_norm:
        g_ref, o_ref, hn_ref = rest[pos:]
        hn_ref[...] = _rms(acc, g_ref[...]).astype(hn_ref.dtype)
    else:
        o_ref = rest[pos]
    o_ref[...] = acc.astype(o_ref.dtype)


def matmul(a, b, *, n=None, col_off=0, res=None, norm_g=None, out_dtype=F32, tm=1024, tn=1024):
    m, k = a.shape
    n = b.shape[1] if n is None else n
    if norm_g is not None:
        tm, tn = min(tm, 512), n
    tm, tn = min(tm, m), min(tn, n)
    assert m % tm == 0 and n % tn == 0 and col_off % tn == 0
    off = col_off // tn
    o_spec = pl.BlockSpec((tm, tn), lambda i, j: (i, j))
    in_specs = [pl.BlockSpec((tm, k), lambda i, j: (i, 0)),
                pl.BlockSpec((k, tn), lambda i, j: (0, j + off))]
    args = [a, b]
    if res is not None:
        in_specs.append(o_spec)
        args.append(res)
    out_specs, out_shape = o_spec, jax.ShapeDtypeStruct((m, n), out_dtype)
    if norm_g is not None:
        in_specs.append(pl.BlockSpec((1, n), lambda i, j: (0, 0)))
        args.append(norm_g.reshape(1, n))
        out_specs, out_shape = [o_spec, o_spec], [out_shape, jax.ShapeDtypeStruct((m, n), BF16)]
    return pl.pallas_call(
        functools.partial(_mm_kernel, has_res=res is not None, has_norm=norm_g is not None),
        grid=(m // tm, n // tn),
        in_specs=in_specs,
        out_specs=out_specs,
        out_shape=out_shape,
        compiler_params=_cparams(("parallel", "parallel")),
        name="matmul",
    )(*args)


def _wkv_kernel(r_ref, k_ref, v_ref, hw_ref, ha_ref, w2_ref, a2_ref, w0_ref, a0_ref, kk_ref, ka_ref, rk_ref,
                y_ref, bonus_ref, state_ref, pre_ref, *, n_groups):
    L = CHUNK
    d = pl.program_id(0)
    sgn = 1 - 2 * d

    @pl.when(pl.program_id(3) == 0)
    def _():
        state_ref[...] = jnp.zeros_like(state_ref)

    row = lax.broadcasted_iota(jnp.int32, (L, GW), 0)
    col = lax.broadcasted_iota(jnp.int32, (L, GW), 1) % L
    diff = (row - col) * sgn
    strict = diff > 0
    incl = diff >= 0
    eye = diff == 0
    level_masks = []
    s = 1
    while s < L:
        level_masks.append(strict & ((row // (2 * s)) == (col // (2 * s))) & ((row // s) != (col // s)))
        s *= 2
    bd_mask = (lax.broadcasted_iota(jnp.int32, (GROUP * L, GW), 0) // L
               == lax.broadcasted_iota(jnp.int32, (GROUP * L, GW), 1) // HEAD)
    seg_ones = bd_mask.astype(BF16)
    tri = (lax.broadcasted_iota(jnp.int32, (L, L), 0) - lax.broadcasted_iota(jnp.int32, (L, L), 1)) * sgn >= 0
    tri = tri.astype(F32)

    heads_per_tile = LANES // HEAD
    tile_lane_head = lax.broadcasted_iota(jnp.int32, (L, LANES), 1) // HEAD
    zero_tile = jnp.zeros((L, LANES), BF16)

    def bd(x):
        xb = x.astype(BF16)
        blocks = []
        for j in range(GROUP):
            tile = j // heads_per_tile
            piece = jnp.where(tile_lane_head == j % heads_per_tile,
                              xb[:, tile * LANES:(tile + 1) * LANES], jnp.zeros((), BF16))
            blocks.append(jnp.concatenate([piece if t == tile else zero_tile for t in range(GW // LANES)], axis=1))
        return jnp.concatenate(blocks, axis=0)

    def mm(x, yb):
        return jnp.dot(x.astype(BF16), yb, preferred_element_type=F32)

    def bdmm(x, y):
        return mm(x, bd(y))

    def fold(full):
        fm = jnp.where(bd_mask, full, 0.0)
        out = fm[0:HEAD]
        for j in range(1, GROUP):
            out = out + fm[j * HEAD:(j + 1) * HEAD]
        return out

    groups = range(n_groups)
    sls = [slice(g * GW, (g + 1) * GW) for g in groups]

    def seg_sum(xs):
        out = _dot(jnp.concatenate(xs, axis=0), seg_ones)
        return [out[g * L:(g + 1) * L] for g in groups]

    def chunk(rows):
        r = [r_ref[0, rows, sl].astype(F32) for sl in sls]
        k = [k_ref[0, rows, sl].astype(F32) for sl in sls]
        v = [v_ref[0, rows, sl].astype(F32) for sl in sls]
        lw = [pre_ref[0, rows, sl] for sl in sls]
        a_s = [pre_ref[1, rows, sl] for sl in sls]
        split = [_split2(x) for x in lw]
        c = [_dot(tri, hi) + _dot(tri, lo) for hi, lo in split]
        kkr = [k[g] * kk_ref[:, sls[g]] for g in groups]
        ss = seg_sum([x * x for x in kkr])
        kk = [kkr[g] / jnp.maximum(jnp.sqrt(ss[g]), 1e-12) for g in groups]
        kd = [k[g] * (1.0 + (a_s[g] - 1.0) * ka_ref[:, sls[g]]) for g in groups]
        b_vec = [kk[g] * a_s[g] for g in groups]
        rkd = seg_sum([r[g] * kd[g] * rk_ref[:, sls[g]] for g in groups])
        bonus_ref[0, 0, rows, :] = jnp.concatenate([rkd[g] * v[g] for g in groups], axis=1).astype(bonus_ref.dtype)

        tot = [jnp.sum(x, axis=0, keepdims=True) for x in lw]
        e_nc = [jnp.exp(-x) for x in c]
        e_tc = [jnp.exp(tot[g] - c[g]) for g in groups]
        at = [-kk[g] * jnp.exp(c[g] - lw[g]) for g in groups]
        rt = [r[g] * jnp.exp(c[g]) for g in groups]
        bt = [b_vec[g] * e_nc[g] for g in groups]
        kt = [kd[g] * e_nc[g] for g in groups]
        bh = [b_vec[g] * e_tc[g] for g in groups]
        kh = [kd[g] * e_tc[g] for g in groups]

        ar = [jnp.concatenate([at[g], rt[g]], axis=0) for g in groups]
        pb = [_dot_nt(ar[g], bd(bt[g])) for g in groups]
        pk = [_dot_nt(ar[g], bd(kt[g])) for g in groups]
        a_ab = [jnp.where(strict, x[:L], 0.0) for x in pb]
        a_rb = [jnp.where(incl, x[L:], 0.0) for x in pb]
        a_ak = [jnp.where(strict, x[:L], 0.0) for x in pk]
        a_rk = [jnp.where(incl, x[L:], 0.0) for x in pk]
        av = [mm(jnp.concatenate([a_ak[g], a_rk[g]], axis=0), bd(v[g])) for g in groups]
        x_loc = [x[:L] for x in av]

        t_inv = [jnp.where(eye, 1.0, 0.0) + jnp.where(level_masks[0], a, 0.0) for a in a_ab]
        for m in level_masks[1:]:
            t1 = [bdmm(t_inv[g], jnp.where(m, a_ab[g], 0.0)) for g in groups]
            t_inv = [t_inv[g] + bdmm(t1[g], t_inv[g]) for g in groups]

        u_loc = [bdmm(t_inv[g], x_loc[g]) for g in groups]
        ta = [bdmm(t_inv[g], at[g]) for g in groups]
        gb = [fold(_dot_tn(bh[g], ta[g])) for g in groups]
        hm = [fold(_dot_tn(jnp.concatenate([bh[g], kh[g]], axis=0), jnp.concatenate([u_loc[g], v[g]], axis=0)))
              for g in groups]
        q = [rt[g] + bdmm(a_rb[g], ta[g]) for g in groups]
        y_loc = [bdmm(a_rb[g], u_loc[g]) + av[g][L:] for g in groups]

        m0 = [state_ref[g] for g in groups]
        qg = [mm(jnp.concatenate([q[g], gb[g]], axis=0), bd(m0[g])) for g in groups]
        y_ref[0, 0, rows, :] = jnp.concatenate([qg[g][:L] + y_loc[g] for g in groups], axis=1).astype(y_ref.dtype)
        e_diag = [_split2(jnp.where(eye, jnp.broadcast_to(jnp.exp(tot[g]), (L, GW)), 0.0)) for g in groups]
        w_hi = seg_sum([x[0] for x in e_diag])
        w_lo = seg_sum([x[1] for x in e_diag])
        for g in groups:
            state_ref[g] = (w_hi[g] + w_lo[g]) * m0[g] + qg[g][L:] + hm[g]

    u = -(_dot(hw_ref[0, 0], w2_ref[0]) + w0_ref[0])
    pre_ref[0] = -jnp.exp(-(jnp.maximum(u, 0.0) + jnp.log(1.0 + jnp.exp(-jnp.abs(u)))) - 0.5)
    pre_ref[1] = 1.0 / (1.0 + jnp.exp(-(_dot(ha_ref[0, 0], a2_ref[0]) + a0_ref[0])))

    n_sub = r_ref.shape[1] // L
    for sub in range(n_sub):
        chunk(pl.ds(pl.multiple_of(jnp.where(d == 0, sub, n_sub - 1 - sub) * L, L), L))


def wkv_bidir(r, k, v, hw, ha, w2, a2, w0, a0, k_k, k_a, r_k, *, n_groups=16):
    bsz, seq, dm = r.shape
    rank = hw.shape[-1]
    assert CHUNK == HEAD and seq % CHUNK == 0
    n_groups = min(n_groups, dm // GW)
    w = n_groups * GW
    assert dm % w == 0
    rows = WKV_STEP_CHUNKS * CHUNK if seq % (WKV_STEP_CHUNKS * CHUNK) == 0 else CHUNK
    nc = seq // rows

    def cidx(d, c):
        return c + d * (nc - 1 - 2 * c)

    x_spec = pl.BlockSpec((1, rows, w), lambda d, b, j, c: (b, cidx(d, c), j))
    d_spec = pl.BlockSpec((1, 1, rows, w), lambda d, b, j, c: (d, b, cidx(d, c), j))
    p_spec = pl.BlockSpec((1, w), lambda d, b, j, c: (0, j))
    h_spec = pl.BlockSpec((1, 1, rows, rank), lambda d, b, j, c: (d, b, cidx(d, c), 0))
    w2_spec = pl.BlockSpec((1, rank, w), lambda d, b, j, c: (d, 0, j))
    b_spec = pl.BlockSpec((1, 1, w), lambda d, b, j, c: (d, 0, j))
    out_sd = jax.ShapeDtypeStruct((2, bsz, seq, dm), BF16)
    return pl.pallas_call(
        functools.partial(_wkv_kernel, n_groups=n_groups),
        grid=(2, bsz, dm // w, nc),
        in_specs=[x_spec, x_spec, x_spec, h_spec, h_spec, w2_spec, w2_spec, b_spec, b_spec,
                  p_spec, p_spec, p_spec],
        out_specs=[d_spec, d_spec],
        out_shape=[out_sd, out_sd],
        scratch_shapes=[pltpu.VMEM((n_groups, HEAD, GW), F32), pltpu.VMEM((2, rows, w), F32)],
        compiler_params=_cparams(("parallel", "parallel", "parallel", "arbitrary")),
        name="wkv_bidir",
    )(r, k, v, hw, ha, w2, a2, w0.reshape(2, 1, dm), a0.reshape(2, 1, dm),
      k_k.reshape(1, dm), k_a.reshape(1, dm), r_k.reshape(1, dm))


def _rms(x, g):
    return x * lax.rsqrt(jnp.mean(x * x, axis=-1, keepdims=True) + RMS_EPS) * g


def _sigmoid(x):
    return 1.0 / (1.0 + jnp.exp(-x))


def _rmsnorm_kernel(x_ref, g_ref, o_ref):
    o_ref[...] = _rms(x_ref[...], g_ref[...]).astype(o_ref.dtype)


def rmsnorm(x, g, *, out_dtype, tm=512):
    m, dm = x.shape
    tm = min(tm, m)
    return pl.pallas_call(
        _rmsnorm_kernel,
        grid=(m // tm,),
        in_specs=[pl.BlockSpec((tm, dm), lambda i: (i, 0)), pl.BlockSpec((1, dm), lambda i: (0, 0))],
        out_specs=pl.BlockSpec((tm, dm), lambda i: (i, 0)),
        out_shape=jax.ShapeDtypeStruct((m, dm), out_dtype),
        compiler_params=_cparams(("parallel",)),
        name="rmsnorm",
    )(x, g.reshape(1, dm))


def _rwkv_pre_kernel(x_ref, xp_ref, xn_ref, g_ref, mu_ref, *o_refs, ts):
    i = pl.program_id(1)
    g = g_ref[...]
    hn = _rms(x_ref[0], g)
    h_before = jnp.where(i == 0, 0.0, _rms(xp_ref[0], g)[7:8])
    h_after = jnp.where(i == pl.num_programs(1) - 1, 0.0, _rms(xn_ref[0], g)[0:1])
    row = lax.broadcasted_iota(jnp.int32, hn.shape, 0)
    prev = jnp.where(row == 0, h_before, pltpu.roll(hn, 1, 0))
    nxt = jnp.where(row == ts - 1, h_after, pltpu.roll(hn, ts - 1, 0))
    delta = 0.5 * (prev + nxt) - hn
    for n, o_ref in enumerate(o_refs):
        o_ref[0] = (hn + delta * mu_ref[n:n + 1, :]).astype(o_ref.dtype)


def rwkv_pre(h, g, mu, *, ts=256):
    bsz, seq, dm = h.shape
    ts = min(ts, seq)
    n_mix = mu.shape[0]
    sub = 8
    nsub = ts // sub
    x_spec = pl.BlockSpec((1, ts, dm), lambda b, i: (b, i, 0))
    return pl.pallas_call(
        functools.partial(_rwkv_pre_kernel, ts=ts),
        grid=(bsz, seq // ts),
        in_specs=[x_spec,
                  pl.BlockSpec((1, sub, dm), lambda b, i: (b, jnp.maximum(i * nsub - 1, 0), 0)),
                  pl.BlockSpec((1, sub, dm), lambda b, i: (b, jnp.minimum((i + 1) * nsub, seq // sub - 1), 0)),
                  pl.BlockSpec((1, dm), lambda b, i: (0, 0)),
                  pl.BlockSpec((n_mix, dm), lambda b, i: (0, 0))],
        out_specs=[x_spec] * n_mix,
        out_shape=[jax.ShapeDtypeStruct((bsz, seq, dm), BF16)] * n_mix,
        compiler_params=_cparams(("parallel", "parallel")),
        name="rwkv_pre",
    )(h, h, h, g.reshape(1, dm), mu)


def _lora_kernel(x_ref, w1_ref, o_ref, *, act):
    hid = _dot(x_ref[...], w1_ref[0])
    if act == "tanh":
        hid = jnp.tanh(hid)
    elif act == "sigmoid":
        hid = _sigmoid(hid)
    o_ref[0] = hid.astype(o_ref.dtype)


def _pad_rank(w1, w2):
    rpad = -w1.shape[-1] % LANES
    return (jnp.pad(w1, ((0, 0), (0, 0), (0, rpad))).astype(BF16),
            jnp.pad(w2, ((0, 0), (0, rpad), (0, 0))).astype(BF16))


def lora_hidden(x, w1, *, act, tm=1024):
    m, dm = x.shape
    nd, _, rp = w1.shape
    tm = min(tm, m)
    return pl.pallas_call(
        functools.partial(_lora_kernel, act=act),
        grid=(nd, m // tm),
        in_specs=[pl.BlockSpec((tm, dm), lambda d, i: (i, 0)),
                  pl.BlockSpec((1, dm, rp), lambda d, i: (d, 0, 0))],
        out_specs=pl.BlockSpec((1, tm, rp), lambda d, i: (d, i, 0)),
        out_shape=jax.ShapeDtypeStruct((nd, m, rp), BF16),
        compiler_params=_cparams(("parallel", "parallel")),
        name="lora_" + act,
    )(x, w1)


def _rwkv_post_kernel(y_ref, b_ref, hg_ref, g2_ref, lw_ref, lb_ref, o_ref):
    dm = o_ref.shape[-1]
    seg = (lax.broadcasted_iota(jnp.int32, (GW, GW), 0) // HEAD
           == lax.broadcasted_iota(jnp.int32, (GW, GW), 1) // HEAD).astype(BF16)
    hg = hg_ref[0]
    for s in range(dm // GW):
        sl = slice(s * GW, (s + 1) * GW)
        y = y_ref[0, :, sl].astype(F32) + y_ref[1, :, sl].astype(F32)
        mean = _dot(y, seg) * (1.0 / HEAD)
        yc = y - mean
        var = _dot(yc * yc, seg) * (1.0 / HEAD)
        yn = yc * lax.rsqrt(var + GN_EPS) * lw_ref[:, sl] + lb_ref[:, sl]
        out = yn + (b_ref[0, :, sl].astype(F32) + b_ref[1, :, sl].astype(F32))
        gate = _dot(hg, g2_ref[:, sl])
        o_ref[:, sl] = (out * gate).astype(o_ref.dtype)


def rwkv_post(y, bonus, hg, g2, ln_w, ln_b, *, tm=512):
    _, m, dm = y.shape
    rank = hg.shape[-1]
    tm = min(tm, m)
    yspec = pl.BlockSpec((2, tm, dm), lambda i: (0, i, 0))
    pspec = pl.BlockSpec((1, dm), lambda i: (0, 0))
    return pl.pallas_call(
        _rwkv_post_kernel,
        grid=(m // tm,),
        in_specs=[yspec, yspec, pl.BlockSpec((1, tm, rank), lambda i: (0, i, 0)),
                  pl.BlockSpec((rank, dm), lambda i: (0, 0)), pspec, pspec],
        out_specs=pl.BlockSpec((tm, dm), lambda i: (i, 0)),
        out_shape=jax.ShapeDtypeStruct((m, dm), BF16),
        compiler_params=_cparams(("parallel",)),
        name="rwkv_post",
    )(y, bonus, hg, g2, ln_w.reshape(1, dm), ln_b.reshape(1, dm))


def _ffn_kernel(be_ref, nu_ref, nq_ref, x_ref, wg_ref, wu_ref, wd_ref, *rest, has_res, has_norm):
    del be_ref, nu_ref
    tm = x_ref.shape[0]
    res_ref = rest[0] if has_res else None
    g_ref = rest[int(has_res)] if has_norm else None
    o_ref = rest[-2] if has_norm else rest[-1]
    f = pl.program_id(1)

    @pl.when(f == 0)
    def _():
        o_ref[...] = res_ref[...] if has_res else jnp.zeros_like(o_ref)

    n_quarters = nq_ref[pl.program_id(0)]
    for q in range(1, ROW_SPLIT + 1):
        rows = q * tm // ROW_SPLIT

        @pl.when(n_quarters == q)
        def _(rows=rows):
            x = x_ref[:rows].astype(BF16)
            gate = _dot(x, wg_ref[0])
            up = _dot(x, wu_ref[0])
            o_ref[:rows] += _dot(gate * _sigmoid(gate) * up, wd_ref[0])

    if has_norm:
        @pl.when(f == pl.num_programs(1) - 1)
        def _():
            rest[-1][...] = _rms(o_ref[...], g_ref[...]).astype(rest[-1].dtype)


def ffn(x, blk_e, n_used, n_quarters, wg, wu, wd, *, res=None, norm_g=None, tm, tf):
    p, dm = x.shape
    fdim = wg.shape[-1]
    assert p % tm == 0 and fdim % tf == 0 and tm % (ROW_SPLIT * 16) == 0
    nf = fdim // tf

    def blk(i, nu):
        return jnp.minimum(i, nu[0] - 1)

    def fblk(i, f, nu):
        return jnp.where(i < nu[0], f, nf - 1)

    row_spec = pl.BlockSpec((tm, dm), lambda i, f, be, nu, nq: (i, 0))
    in_specs = [pl.BlockSpec((tm, dm), lambda i, f, be, nu, nq: (blk(i, nu), 0)),
                pl.BlockSpec((1, dm, tf), lambda i, f, be, nu, nq: (be[blk(i, nu)], 0, fblk(i, f, nu))),
                pl.BlockSpec((1, dm, tf), lambda i, f, be, nu, nq: (be[blk(i, nu)], 0, fblk(i, f, nu))),
                pl.BlockSpec((1, tf, dm), lambda i, f, be, nu, nq: (be[blk(i, nu)], fblk(i, f, nu), 0))]
    args = [x, wg, wu, wd]
    if res is not None:
        in_specs.append(row_spec)
        args.append(res)
    out_specs, out_shape = row_spec, jax.ShapeDtypeStruct((p, dm), F32)
    if norm_g is not None:
        in_specs.append(pl.BlockSpec((1, dm), lambda i, f, be, nu, nq: (0, 0)))
        args.append(norm_g.reshape(1, dm))
        out_specs, out_shape = [row_spec, row_spec], [out_shape, jax.ShapeDtypeStruct((p, dm), BF16)]
    return pl.pallas_call(
        functools.partial(_ffn_kernel, has_res=res is not None, has_norm=norm_g is not None),
        grid_spec=pltpu.PrefetchScalarGridSpec(
            num_scalar_prefetch=3, grid=(p // tm, nf),
            in_specs=in_specs, out_specs=out_specs),
        out_shape=out_shape,
        compiler_params=_cparams(("parallel", "arbitrary")),
        name="ffn",
    )(blk_e, n_used, n_quarters, *args)


def _rope_table_kernel(pos_ref, inv_ref, cos_ref, sin_ref):
    ang = pos_ref[...].astype(F32) * inv_ref[...]
    lane = lax.broadcasted_iota(jnp.int32, ang.shape, 1)
    cos_ref[...] = jnp.cos(ang)
    sin_ref[...] = jnp.where((lane % HEAD) < HEAD // 2, -jnp.sin(ang), jnp.sin(ang))


def rope_table(positions, *, tm=1024):
    m = positions.shape[0]
    tm = min(tm, m)
    half = HEAD // 2
    inv = ROPE_THETA ** (-(2.0 * (jnp.arange(LANES) % half)).astype(F32) / HEAD)
    spec = pl.BlockSpec((tm, LANES), lambda i: (i, 0))
    return pl.pallas_call(
        _rope_table_kernel,
        grid=(m // tm,),
        in_specs=[pl.BlockSpec((tm, 1), lambda i: (i, 0)), pl.BlockSpec((1, LANES), lambda i: (0, 0))],
        out_specs=[spec, spec],
        out_shape=[jax.ShapeDtypeStruct((m, LANES), F32)] * 2,
        compiler_params=_cparams(("parallel",)),
        name="rope_table",
    )(positions.reshape(m, 1), inv.reshape(1, LANES))


def _proj_rope_kernel(a_ref, b_ref, cos_ref, sin_ref, o_ref, *, scale, n_sub):
    tm, tn = o_ref.shape
    sub = tm // n_sub
    lane = lax.broadcasted_iota(jnp.int32, (sub, LANES), 1)
    first = (lane % HEAD) < HEAD // 2

    def project(i):
        return _dot(a_ref[i * sub:(i + 1) * sub, :], b_ref[...])

    def rotate(i, acc):
        rows = slice(i * sub, (i + 1) * sub)
        cos = cos_ref[rows, :]
        sin_signed = sin_ref[rows, :]
        for s in range(tn // LANES):
            sl = slice(s * LANES, (s + 1) * LANES)
            t = acc[:, sl]
            partner = jnp.where(first, pltpu.roll(t, LANES - HEAD // 2, 1), pltpu.roll(t, HEAD // 2, 1))
            o_ref[rows, sl] = ((t * cos + partner * sin_signed) * scale).astype(o_ref.dtype)

    acc = project(0)
    for i in range(1, n_sub):
        nxt = project(i)
        rotate(i - 1, acc)
        acc = nxt
    rotate(n_sub - 1, acc)


def _proj_ones_kernel(a_ref, b_ref, o_ref):
    acc = _dot(a_ref[...], b_ref[...])
    lane = lax.broadcasted_iota(jnp.int32, (acc.shape[0], LANES), 1)
    ones_col = jnp.where(lane == 0, 1.0, 0.0).astype(o_ref.dtype)
    for s in range(acc.shape[1] // LANES):
        o_ref[:, 2 * s * LANES:(2 * s + 1) * LANES] = acc[:, s * LANES:(s + 1) * LANES].astype(o_ref.dtype)
        o_ref[:, (2 * s + 1) * LANES:(2 * s + 2) * LANES] = ones_col


def qkv_rope(hn, w_qkv, cos, sin_signed, *, tm=1024, tn=512):
    m, dm = hn.shape
    assert 2 * HEAD == LANES
    tm, tn = min(tm, m), min(tn, dm)
    nb = dm // tn
    a_spec = pl.BlockSpec((tm, dm), lambda i, j: (i, 0))
    t_spec = pl.BlockSpec((tm, LANES), lambda i, j: (i, 0))

    def b_spec(part):
        return pl.BlockSpec((dm, tn), lambda i, j: (0, j + part * nb))

    def rope_proj(part, scale):
        return pl.pallas_call(
            functools.partial(_proj_rope_kernel, scale=scale, n_sub=4 if tm % 64 == 0 else 1),
            grid=(m // tm, nb),
            in_specs=[a_spec, b_spec(part), t_spec, t_spec],
            out_specs=pl.BlockSpec((tm, tn), lambda i, j: (i, j)),
            out_shape=jax.ShapeDtypeStruct((m, dm), BF16),
            compiler_params=_cparams(("parallel", "parallel")),
            name="proj_rope",
        )(hn, w_qkv, cos, sin_signed)

    q = rope_proj(0, HEAD ** -0.5 * LOG2E)
    k = rope_proj(1, 1.0)
    v1 = pl.pallas_call(
        _proj_ones_kernel,
        grid=(m // tm, nb),
        in_specs=[a_spec, b_spec(2)],
        out_specs=pl.BlockSpec((tm, 2 * tn), lambda i, j: (i, j)),
        out_shape=jax.ShapeDtypeStruct((m, 2 * dm), BF16),
        compiler_params=_cparams(("parallel", "parallel")),
        name="proj_ones",
    )(hn, w_qkv)
    return q, k, v1


def _attn_kernel(q_ref, k_ref, v_ref, lq1_ref, lk1_ref, lq2_ref, lk2_ref, sub_ref, o_ref,
                 m_ref, acc_ref, s_ref, *, lambda_init, tk):
    tq, hw = q_ref.shape
    seq = k_ref.shape[0]
    m_ref[...] = jnp.full_like(m_ref, -jnp.inf)
    acc_ref[...] = jnp.zeros_like(acc_ref)
    q = q_ref[...]
    lane = lax.broadcasted_iota(jnp.int32, q.shape, 1)
    qs = [jnp.where((lane >= c * HEAD) & (lane < (c + 1) * HEAD), q, jnp.zeros((), q.dtype)) for c in range(2)]

    def rows(j):
        return pl.ds(pl.multiple_of(j * tk, tk), tk)

    def scores(j, slot):
        ks = k_ref[rows(j), :]
        for c in range(2):
            s_ref[slot, c] = _dot_nt(qs[c], ks)

    def accumulate(j, slot):
        vs = v_ref[rows(j), :]
        for c in range(2):
            s = s_ref[slot, c]
            m_prev = m_ref[c]
            m_new = jnp.maximum(m_prev, jnp.max(s, axis=1, keepdims=True))
            alpha = jnp.exp2(m_prev - m_new)
            p = jnp.exp2(s - jnp.concatenate([m_new] * (tk // hw), axis=1))
            acc_ref[c] = acc_ref[c] * jnp.concatenate([alpha, alpha], axis=1) + _dot(p, vs)
            m_ref[c] = m_new

    n_chunks = seq // tk

    def body(jj, carry):
        scores(2 * jj + 1, 1)
        accumulate(2 * jj, 0)
        scores(2 * jj + 2, 0)
        accumulate(2 * jj + 1, 1)
        return carry

    scores(0, 0)
    lax.fori_loop(0, n_chunks // 2 - 1, body, 0)
    scores(n_chunks - 1, 1)
    accumulate(n_chunks - 2, 0)
    accumulate(n_chunks - 1, 1)
    lam = (jnp.exp(jnp.sum(lq1_ref[...] * lk1_ref[...], keepdims=True))
           - jnp.exp(jnp.sum(lq2_ref[...] * lk2_ref[...], keepdims=True)) + lambda_init)
    a1 = acc_ref[0]
    a2 = acc_ref[1]
    o = a1[:, :hw] / a1[:, hw:hw + 1] - lam * (a2[:, :hw] / a2[:, hw:hw + 1])
    o = o * lax.rsqrt(jnp.mean(o * o, axis=-1, keepdims=True) + SUBLN_EPS) * sub_ref[...] * (1.0 - lambda_init)
    o_ref[...] = o.astype(o_ref.dtype)


def diff_attn(q, k, v1, lq1, lk1, lq2, lk2, subln, lambda_init, *, bsz, tq=512, tk=2048):
    m, dm = q.shape
    seq = m // bsz
    hw = 2 * HEAD
    tq, tk = min(tq, seq), min(tk, seq // 2)
    assert seq % tq == 0 and seq % (2 * tk) == 0
    nq = seq // tq
    q_spec = pl.BlockSpec((tq, hw), lambda b, h, i: (b * nq + i, h))
    l_spec = pl.BlockSpec((1, HEAD), lambda b, h, i: (0, 0))
    return pl.pallas_call(
        functools.partial(_attn_kernel, lambda_init=lambda_init, tk=tk),
        grid=(bsz, dm // hw, nq),
        in_specs=[q_spec, pl.BlockSpec((seq, hw), lambda b, h, i: (b, h)),
                  pl.BlockSpec((seq, 2 * hw), lambda b, h, i: (b, h)), l_spec, l_spec, l_spec, l_spec,
                  pl.BlockSpec((1, hw), lambda b, h, i: (0, 0))],
        out_specs=q_spec,
        out_shape=jax.ShapeDtypeStruct((m, dm), BF16),
        scratch_shapes=[pltpu.VMEM((2, tq, hw), F32), pltpu.VMEM((2, tq, 2 * hw), F32),
                        pltpu.VMEM((2, 2, tq, tk), F32)],
        compiler_params=_cparams(("parallel", "parallel", "parallel")),
        name="diff_attn",
    )(q, k, v1, lq1.reshape(1, HEAD), lk1.reshape(1, HEAD), lq2.reshape(1, HEAD), lk2.reshape(1, HEAD),
      subln.reshape(1, hw))


def _router_kernel(h_ref, g_ref, wr_ref, hn_ref, e_ref, gate_ref, rank_ref, cnt_ref, *, n_exp):
    tm = h_ref.shape[0]

    @pl.when(pl.program_id(0) == 0)
    def _():
        cnt_ref[...] = jnp.zeros_like(cnt_ref)

    hn = _rms(h_ref[...], g_ref[...])
    hn_ref[...] = hn
    lane = lax.broadcasted_iota(jnp.int32, (tm, LANES), 1)
    logits = jnp.where(lane < n_exp, _dot_f32(hn, wr_ref[...]), -jnp.inf)
    m1 = jnp.max(logits, axis=1, keepdims=True)
    i1 = jnp.min(jnp.where(logits == m1, lane, LANES), axis=1, keepdims=True)
    rest = jnp.where(lane == i1, -jnp.inf, logits)
    m2 = jnp.max(rest, axis=1, keepdims=True)
    i2 = jnp.min(jnp.where(rest == m2, lane, LANES), axis=1, keepdims=True)
    e21 = jnp.exp(m2 - m1)
    g1 = 1.0 / (1.0 + e21)
    g2 = e21 / (1.0 + e21)
    oh1 = lane == i1
    oh2 = lane == i2
    both = jnp.where(oh1 | oh2, 1.0, 0.0)
    earlier = (lax.broadcasted_iota(jnp.int32, (tm, tm), 0) > lax.broadcasted_iota(jnp.int32, (tm, tm), 1))
    cum = _dot(earlier.astype(BF16), both) + cnt_ref[...]
    r1 = jnp.sum(jnp.where(oh1, cum, 0.0), axis=1, keepdims=True)
    r2 = jnp.sum(jnp.where(oh2, cum, 0.0), axis=1, keepdims=True)
    cnt_ref[...] += jnp.sum(both, axis=0, keepdims=True)
    e_ref[...] = jnp.where(lane == 0, i1, jnp.where(lane == 1, i2, 0))
    gate_ref[...] = jnp.where(lane == 0, g1, jnp.where(lane == 1, g2, 0.0))
    rank_ref[...] = jnp.where(lane == 0, r1, jnp.where(lane == 1, r2, 0.0)).astype(jnp.int32)


def moe_route(h, g, w_router, *, tm=512):
    m, dm = h.shape
    n_exp = w_router.shape[1]
    tm = min(tm, m)
    wr = jnp.pad(w_router, ((0, 0), (0, LANES - n_exp)))
    row = lambda dt: jax.ShapeDtypeStruct((m, LANES), dt)
    lspec = pl.BlockSpec((tm, LANES), lambda i: (i, 0))
    return pl.pallas_call(
        functools.partial(_router_kernel, n_exp=n_exp),
        grid=(m // tm,),
        in_specs=[pl.BlockSpec((tm, dm), lambda i: (i, 0)), pl.BlockSpec((1, dm), lambda i: (0, 0)),
                  pl.BlockSpec((dm, LANES), lambda i: (0, 0))],
        out_specs=[pl.BlockSpec((tm, dm), lambda i: (i, 0)), lspec, lspec, lspec,
                   pl.BlockSpec((1, LANES), lambda i: (0, 0))],
        out_shape=[jax.ShapeDtypeStruct((m, dm), F32), row(jnp.int32), row(F32), row(jnp.int32),
                   jax.ShapeDtypeStruct((1, LANES), F32)],
        compiler_params=_cparams(("arbitrary",)),
        name="moe_route",
    )(h, g.reshape(1, dm), wr)


def _dest_kernel(e_ref, rank_ref, cnt_ref, dest_ref, be_ref, *, n_exp, bm):
    tm = e_ref.shape[0]
    nb = be_ref.shape[0]
    cnt = cnt_ref[...]
    pcnt = jnp.floor((cnt + (bm - 1)) * (1.0 / bm)) * bm
    upper = (lax.broadcasted_iota(jnp.int32, (LANES, LANES), 0)
             <= lax.broadcasted_iota(jnp.int32, (LANES, LANES), 1)).astype(F32)
    pends = _dot_f32(jnp.broadcast_to(pcnt, (8, LANES)), upper)[0:1]
    pstart = pends - pcnt
    lane = lax.broadcasted_iota(jnp.int32, (tm, LANES), 1)
    e = e_ref[...]
    d1 = jnp.sum(jnp.where(lane == e[:, 0:1], pstart, 0.0), axis=1, keepdims=True)
    d2 = jnp.sum(jnp.where(lane == e[:, 1:2], pstart, 0.0), axis=1, keepdims=True)
    dest = jnp.where(lane == 0, d1, jnp.where(lane == 1, d2, 0.0)).astype(jnp.int32) + rank_ref[...]
    dest_ref[...] = dest
    blk_lane = lax.broadcasted_iota(jnp.int32, (nb, LANES), 1)
    blk_row = lax.broadcasted_iota(jnp.int32, (nb, LANES), 0).astype(F32) * bm
    ended = jnp.where((blk_lane < n_exp) & (pends <= blk_row), 1.0, 0.0)
    be = jnp.minimum(jnp.sum(ended, axis=1, keepdims=True), n_exp - 1.0)
    n_used = jnp.sum(jnp.where(blk_lane == n_exp - 1, pends, 0.0), axis=1, keepdims=True) * (1.0 / bm)
    seg_end = jnp.sum(jnp.where(blk_lane.astype(F32) == be, pstart + cnt, 0.0), axis=1, keepdims=True)
    valid = jnp.clip(seg_end - blk_row[:, 0:1], 0.0, float(bm))
    quarters = jnp.ceil(valid * (ROW_SPLIT / bm))
    be_ref[...] = jnp.where(blk_lane == 1, n_used, jnp.where(blk_lane == 2, quarters, be)).astype(jnp.int32)


def moe_dest(e, rank, cnt, *, n_exp, bm, nblk, tm=512):
    m = e.shape[0]
    tm = min(tm, m)
    nb = -(-nblk // 8) * 8
    lspec = pl.BlockSpec((tm, LANES), lambda i: (i, 0))
    return pl.pallas_call(
        functools.partial(_dest_kernel, n_exp=n_exp, bm=bm),
        grid=(m // tm,),
        in_specs=[lspec, lspec, pl.BlockSpec((1, LANES), lambda i: (0, 0))],
        out_specs=[lspec, pl.BlockSpec((nb, LANES), lambda i: (0, 0))],
        out_shape=[jax.ShapeDtypeStruct((m, LANES), jnp.int32), jax.ShapeDtypeStruct((nb, LANES), jnp.int32)],
        compiler_params=_cparams(("arbitrary",)),
        name="moe_dest",
    )(e, rank, cnt)


def _row_copy(src, dst, sem):
    return pltpu.make_async_copy(src, dst, sem)


def _dispatch_kernel(dest_ref, hn_ref, xs_in_ref, xs_ref, sem):
    del xs_in_ref
    tm = hn_ref.shape[0]
    base = pl.program_id(0) * tm

    def issue(r, carry):
        for s in range(TOP_K):
            dst = dest_ref[TOP_K * (base + r) + s]
            _row_copy(hn_ref.at[pl.ds(r, 1)], xs_ref.at[pl.ds(dst, 1)], sem).start()
        return carry

    lax.fori_loop(0, tm, issue, 0)

    def drain(r, carry):
        for s in range(TOP_K):
            _row_copy(hn_ref.at[pl.ds(0, 1)], xs_ref.at[pl.ds(0, 1)], sem).wait()
        return carry

    lax.fori_loop(0, tm, drain, 0)


def moe_dispatch(dest_flat, hn, n_rows, *, tm=512):
    m, dm = hn.shape
    tm = min(tm, m)
    xs0 = jnp.zeros((n_rows, dm), hn.dtype)
    return pl.pallas_call(
        _dispatch_kernel,
        grid_spec=pltpu.PrefetchScalarGridSpec(
            num_scalar_prefetch=1, grid=(m // tm,),
            in_specs=[pl.BlockSpec((tm, dm), lambda i, d: (i, 0)), pl.BlockSpec(memory_space=pl.ANY)],
            out_specs=pl.BlockSpec(memory_space=pl.ANY),
            scratch_shapes=[pltpu.SemaphoreType.DMA(())]),
        out_shape=jax.ShapeDtypeStruct((n_rows, dm), hn.dtype),
        input_output_aliases={2: 0},
        compiler_params=_cparams(("arbitrary",)),
        name="moe_dispatch",
    )(dest_flat, hn, xs0)


def _combine_kernel(dest_ref, yb_ref, gate_ref, res_ref, *rest, has_norm):
    norm_ref = rest[0] if has_norm else None
    o_ref, buf_ref, sem = rest[-3:]
    tm = res_ref.shape[0]
    base = pl.program_id(0) * tm

    def issue(r, carry):
        for s in range(TOP_K):
            src = dest_ref[TOP_K * (base + r) + s]
            _row_copy(yb_ref.at[pl.ds(src, 1)], buf_ref.at[s, pl.ds(r, 1)], sem).start()
        return carry

    lax.fori_loop(0, tm, issue, 0)

    def drain(r, carry):
        for s in range(TOP_K):
            _row_copy(yb_ref.at[pl.ds(0, 1)], buf_ref.at[s, pl.ds(0, 1)], sem).wait()
        return carry

    lax.fori_loop(0, tm, drain, 0)
    gate = gate_ref[...]
    out = res_ref[...] + (gate[:, 0:1] * buf_ref[0] + gate[:, 1:2] * buf_ref[1])
    o_ref[...] = _rms(out, norm_ref[...]) if norm_ref is not None else out


def moe_combine(dest_flat, yb, gates, res, *, norm_g=None, tm=512):
    m, dm = res.shape
    tm = min(tm, m)
    row_spec = pl.BlockSpec((tm, dm), lambda i, d: (i, 0))
    in_specs = [pl.BlockSpec(memory_space=pl.ANY), pl.BlockSpec((tm, LANES), lambda i, d: (i, 0)), row_spec]
    args = [dest_flat, yb, gates, res]
    if norm_g is not None:
        in_specs.append(pl.BlockSpec((1, dm), lambda i, d: (0, 0)))
        args.append(norm_g.reshape(1, dm))
    return pl.pallas_call(
        functools.partial(_combine_kernel, has_norm=norm_g is not None),
        grid_spec=pltpu.PrefetchScalarGridSpec(
            num_scalar_prefetch=1, grid=(m // tm,),
            in_specs=in_specs,
            out_specs=row_spec,
            scratch_shapes=[pltpu.VMEM((TOP_K, tm, dm), F32), pltpu.SemaphoreType.DMA(())]),
        out_shape=jax.ShapeDtypeStruct((m, dm), F32),
        compiler_params=_cparams(("arbitrary",)),
        name="moe_combine",
    )(*args)


def moe_layer(h, g, w_router, wg, wu, wd, *, norm_g=None):
    m, dm = h.shape
    n_exp = w_router.shape[1]
    bm = 1024 if m * TOP_K >= 8 * 1024 else 128
    n_rows = m * TOP_K + n_exp * bm
    nblk = n_rows // bm
    hn, e, gates, rank, cnt = moe_route(h, g, w_router)
    dest, be = moe_dest(e, rank, cnt, n_exp=n_exp, bm=bm, nblk=nblk)
    dest_flat = dest[:, :TOP_K].reshape(-1)
    xs = moe_dispatch(dest_flat, hn, n_rows)
    tf = 256 if wg.shape[-1] % 256 == 0 else LANES
    yb = ffn(xs, be[:nblk, 0], be[0, 1:2], be[:nblk, 2], wg, wu, wd, tm=bm, tf=tf)
    return moe_combine(dest_flat, yb, gates, h, norm_g=norm_g)


def rwkv_layer(h, bsz, g_norm, mu, w_rkv, g1, g2, w0, w1, w2, a0, a1, a2, k_k, k_a, r_k, ln_w, ln_b, w_o,
               g_next):
    m, dm = h.shape
    seq = m // bsz
    xr, xk, xv, xw, xa, xg = (t.reshape(m, dm) for t in rwkv_pre(h.reshape(bsz, seq, dm), g_norm, mu))
    w_rkv = w_rkv.astype(BF16)
    r, k, v = (matmul(xi, w_rkv, n=dm, col_off=n * dm, out_dtype=BF16).reshape(bsz, seq, dm)
               for n, xi in enumerate((xr, xk, xv)))
    g1, g2 = _pad_rank(g1[None], g2[None])
    w1, w2 = _pad_rank(w1, w2)
    a1, a2 = _pad_rank(a1, a2)
    hg = lora_hidden(xg, g1, act="sigmoid")
    hw = lora_hidden(xw, w1, act="tanh").reshape(2, bsz, seq, -1)
    ha = lora_hidden(xa, a1, act="none").reshape(2, bsz, seq, -1)
    y, bonus = wkv_bidir(r, k, v, hw, ha, w2, a2, w0, a0, k_k, k_a, r_k)
    z = rwkv_post(y.reshape(2, m, dm), bonus.reshape(2, m, dm), hg, g2[0], ln_w, ln_b)
    return matmul(z, w_o.astype(BF16), res=h, norm_g=g_next)


def attn_layer(h, hn, bsz, positions, w_qkv, lq1, lk1, lq2, lk2, subln, w_o, lambda_init):
    cos, sin_signed = rope_table(positions.reshape(-1))
    q, k, v1 = qkv_rope(hn, w_qkv.astype(BF16), cos, sin_signed)
    o = diff_attn(q, k, v1, lq1, lk1, lq2, lk2, subln, lambda_init, bsz=bsz)
    return matmul(o, w_o.astype(BF16), res=h)


def dense_ffn_layer(h, hn, wg, wu, wd, g_next):
    m = h.shape[0]
    tm = min(512, m)
    tf = 1024 if wg.shape[-1] % 1024 == 0 else LANES
    nblk = m // tm
    return ffn(hn, jnp.zeros((nblk,), jnp.int32), jnp.full((1,), nblk, jnp.int32),
               jnp.full((nblk,), ROW_SPLIT, jnp.int32), wg.astype(BF16)[None], wu.astype(BF16)[None],
               wd.astype(BF16)[None], res=h, norm_g=g_next, tm=tm, tf=tf)


def kernel(x, positions, norm_mix, norm_ffn, norm_final, rw_mu, rw_w_rkv, rw_g1, rw_g2, rw_w0, rw_w1, rw_w2, rw_a0, rw_a1, rw_a2, rw_kk, rw_ka, rw_rk, rw_ln_w, rw_ln_b, rw_w_o, da_w_qkv, da_lq1, da_lk1, da_lq2, da_lk2, da_subln, da_w_o, ff_wg, ff_wu, ff_wd, moe_router, moe_wg, moe_wu, moe_wd):
    bsz, seq, dm = x.shape
    depth = norm_mix.shape[0]
    h = x.reshape(bsz * seq, dm)
    hn = None
    for i in range(depth):
        j = i // 2
        if i % 2 == 0:
            h, hn = rwkv_layer(h, bsz, norm_mix[i], rw_mu[j], rw_w_rkv[j], rw_g1[j], rw_g2[j], rw_w0[j], rw_w1[j],
                               rw_w2[j], rw_a0[j], rw_a1[j], rw_a2[j], rw_kk[j], rw_ka[j], rw_rk[j],
                               rw_ln_w[j], rw_ln_b[j], rw_w_o[j], norm_ffn[i])
            if i + 1 < depth:
                h, hn = dense_ffn_layer(h, hn, ff_wg[j], ff_wu[j], ff_wd[j], norm_mix[i + 1])
            else:
                h = dense_ffn_layer(h, hn, ff_wg[j], ff_wu[j], ff_wd[j], None)
        else:
            lambda_init = 0.8 - 0.6 * math.exp(-0.3 * i)
            h = attn_layer(h, hn, bsz, positions, da_w_qkv[j], da_lq1[j], da_lk1[j], da_lq2[j],
                           da_lk2[j], da_subln[j], da_w_o[j], lambda_init)
            last = i == depth - 1
            h = moe_layer(h, norm_ffn[i], moe_router[j], moe_wg[j], moe_wu[j], moe_wd[j],
                          norm_g=norm_final if last else None)
    if depth % 2 == 1 or depth == 0:
        h = rmsnorm(h, norm_final, out_dtype=x.dtype)
    return h.reshape(bsz, seq, dm)
```

```python
import functools
import math

import jax
import jax.numpy as jnp
from jax import lax
from jax.experimental import pallas as pl
from jax.experimental.pallas import tpu as pltpu

F32 = jnp.float32
BF16 = jnp.bfloat16

RMS_EPS = 1e-6
HEAD = 64
GN_EPS = HEAD * 1e-5
ROPE_THETA = 10000.0
SUBLN_EPS = 1e-5
LOG2E = math.log2(math.e)
TOP_K = 2
LANES = 128
CHUNK = 64
GROUP = 2
WKV_STEP_CHUNKS = 2
ATTN_STEP_TILES = 4
ROW_SPLIT = 4
GW = GROUP * HEAD
VMEM_LIMIT = 56 * 1024 * 1024


def _cparams(sem, vmem=VMEM_LIMIT):
    return pltpu.CompilerParams(dimension_semantics=sem, vmem_limit_bytes=vmem)


def _dot(a, b):
    return jnp.dot(a.astype(BF16), b.astype(BF16), preferred_element_type=F32)


def _dot_f32(a, b):
    return jnp.dot(a, b, preferred_element_type=F32, precision=lax.Precision.HIGHEST)


def _split2(x):
    hi = x.astype(BF16)
    return hi, (x - hi.astype(F32)).astype(BF16)


def _dot_nt(a, b):
    return lax.dot_general(a.astype(BF16), b.astype(BF16), (((1,), (1,)), ((), ())),
                           preferred_element_type=F32)


def _dot_tn(a, b):
    return lax.dot_general(a.astype(BF16), b.astype(BF16), (((0,), (0,)), ((), ())),
                           preferred_element_type=F32)


def _mm_kernel(a_ref, b_ref, *rest, has_res, has_norm):
    acc = _dot(a_ref[...], b_ref[...])
    pos = 0
    if has_res:
        acc = acc + rest[pos][...]
        pos += 1
    if has_norm:
        g_ref, o_ref, hn_ref = rest[pos:]
        hn_ref[...] = _rms(acc, g_ref[...]).astype(hn_ref.dtype)
    else:
        o_ref = rest[pos]
    o_ref[...] = acc.astype(o_ref.dtype)


def matmul(a, b, *, n=None, col_off=0, res=None, norm_g=None, out_dtype=F32, tm=1024, tn=1024):
    m, k = a.shape
    n = b.shape[1] if n is None else n
    if norm_g is not None:
        tm, tn = min(tm, 512), n
    tm, tn = min(tm, m), min(tn, n)
    assert m % tm == 0 and n % tn == 0 and col_off % tn == 0
    off = col_off // tn
    o_spec = pl.BlockSpec((tm, tn), lambda i, j: (i, j))
    in_specs = [pl.BlockSpec((tm, k), lambda i, j: (i, 0)),
                pl.BlockSpec((k, tn), lambda i, j: (0, j + off))]
    args = [a, b]
    if res is not None:
        in_specs.append(o_spec)
        args.append(res)
    out_specs, out_shape = o_spec, jax.ShapeDtypeStruct((m, n), out_dtype)
    if norm_g is not None:
        in_specs.append(pl.BlockSpec((1, n), lambda i, j: (0, 0)))
        args.append(norm_g.reshape(1, n))
        out_specs, out_shape = [o_spec, o_spec], [out_shape, jax.ShapeDtypeStruct((m, n), BF16)]
    return pl.pallas_call(
        functools.partial(_mm_kernel, has_res=res is not None, has_norm=norm_g is not None),
        grid=(m // tm, n // tn),
        in_specs=in_specs,
        out_specs=out_specs,
        out_shape=out_shape,
        compiler_params=_cparams(("parallel", "parallel")),
        name="matmul",
    )(*args)


def _wkv_kernel(r_ref, k_ref, v_ref, hw_ref, ha_ref, w2_ref, a2_ref, w0_ref, a0_ref, kk_ref, ka_ref, rk_ref,
                y_ref, bonus_ref, state_ref, pre_ref, *, n_groups):
    L = CHUNK
    d = pl.program_id(0)
    sgn = 1 - 2 * d

    @pl.when(pl.program_id(3) == 0)
    def _():
        state_ref[...] = jnp.zeros_like(state_ref)

    row = lax.broadcasted_iota(jnp.int32, (L, GW), 0)
    col = lax.broadcasted_iota(jnp.int32, (L, GW), 1) % L
    diff = (row - col) * sgn
    strict = diff > 0
    incl = diff >= 0
    eye = diff == 0
    level_masks = []
    s = 1
    while s < L:
        level_masks.append(strict & ((row // (2 * s)) == (col // (2 * s))) & ((row // s) != (col // s)))
        s *= 2
    bd_mask = (lax.broadcasted_iota(jnp.int32, (GROUP * L, GW), 0) // L
               == lax.broadcasted_iota(jnp.int32, (GROUP * L, GW), 1) // HEAD)
    seg_ones = bd_mask.astype(BF16)
    tri = (lax.broadcasted_iota(jnp.int32, (L, L), 0) - lax.broadcasted_iota(jnp.int32, (L, L), 1)) * sgn >= 0
    tri = tri.astype(F32)

    heads_per_tile = LANES // HEAD
    tile_lane_head = lax.broadcasted_iota(jnp.int32, (L, LANES), 1) // HEAD
    zero_tile = jnp.zeros((L, LANES), BF16)

    def bd(x):
        xb = x.astype(BF16)
        blocks = []
        for j in range(GROUP):
            tile = j // heads_per_tile
            piece = jnp.where(tile_lane_head == j % heads_per_tile,
                              xb[:, tile * LANES:(tile + 1) * LANES], jnp.zeros((), BF16))
            blocks.append(jnp.concatenate([piece if t == tile else zero_tile for t in range(GW // LANES)], axis=1))
        return jnp.concatenate(blocks, axis=0)

    def mm(x, yb):
        return jnp.dot(x.astype(BF16), yb, preferred_element_type=F32)

    def bdmm(x, y):
        return mm(x, bd(y))

    def fold(full):
        fm = jnp.where(bd_mask, full, 0.0)
        out = fm[0:HEAD]
        for j in range(1, GROUP):
            out = out + fm[j * HEAD:(j + 1) * HEAD]
        return out

    groups = range(n_groups)
    sls = [slice(g * GW, (g + 1) * GW) for g in groups]

    def seg_sum(xs):
        out = _dot(jnp.concatenate(xs, axis=0), seg_ones)
        return [out[g * L:(g + 1) * L] for g in groups]

    def chunk(rows):
        r = [r_ref[0, rows, sl].astype(F32) for sl in sls]
        k = [k_ref[0, rows, sl].astype(F32) for sl in sls]
        v = [v_ref[0, rows, sl].astype(F32) for sl in sls]
        lw = [pre_ref[0, rows, sl] for sl in sls]
        a_s = [pre_ref[1, rows, sl] for sl in sls]
        split = [_split2(x) for x in lw]
        c = [_dot(tri, hi) + _dot(tri, lo) for hi, lo in split]
        kkr = [k[g] * kk_ref[:, sls[g]] for g in groups]
        ss = seg_sum([x * x for x in kkr])
        kk = [kkr[g] / jnp.maximum(jnp.sqrt(ss[g]), 1e-12) for g in groups]
        kd = [k[g] * (1.0 + (a_s[g] - 1.0) * ka_ref[:, sls[g]]) for g in groups]
        b_vec = [kk[g] * a_s[g] for g in groups]
        rkd = seg_sum([r[g] * kd[g] * rk_ref[:, sls[g]] for g in groups])
        bonus_ref[0, 0, rows, :] = jnp.concatenate([rkd[g] * v[g] for g in groups], axis=1).astype(bonus_ref.dtype)

        tot = [jnp.sum(x, axis=0, keepdims=True) for x in lw]
        e_nc = [jnp.exp(-x) for x in c]
        e_tc = [jnp.exp(tot[g] - c[g]) for g in groups]
        at = [-kk[g] * jnp.exp(c[g] - lw[g]) for g in groups]
        rt = [r[g] * jnp.exp(c[g]) for g in groups]
        bt = [b_vec[g] * e_nc[g] for g in groups]
        kt = [kd[g] * e_nc[g] for g in groups]
        bh = [b_vec[g] * e_tc[g] for g in groups]
        kh = [kd[g] * e_tc[g] for g in groups]

        ar = [jnp.concatenate([at[g], rt[g]], axis=0) for g in groups]
        pb = [_dot_nt(ar[g], bd(bt[g])) for g in groups]
        pk = [_dot_nt(ar[g], bd(kt[g])) for g in groups]
        a_ab = [jnp.where(strict, x[:L], 0.0) for x in pb]
        a_rb = [jnp.where(incl, x[L:], 0.0) for x in pb]
        a_ak = [jnp.where(strict, x[:L], 0.0) for x in pk]
        a_rk = [jnp.where(incl, x[L:], 0.0) for x in pk]
        av = [mm(jnp.concatenate([a_ak[g], a_rk[g]], axis=0), bd(v[g])) for g in groups]
        x_loc = [x[:L] for x in av]

        t_inv = [jnp.where(eye, 1.0, 0.0) + jnp.where(level_masks[0], a, 0.0) for a in a_ab]
        for m in level_masks[1:]:
            t1 = [bdmm(t_inv[g], jnp.where(m, a_ab[g], 0.0)) for g in groups]
            t_inv = [t_inv[g] + bdmm(t1[g], t_inv[g]) for g in groups]

        u_loc = [bdmm(t_inv[g], x_loc[g]) for g in groups]
        ta = [bdmm(t_inv[g], at[g]) for g in groups]
        gb = [fold(_dot_tn(bh[g], ta[g])) for g in groups]
        hm = [fold(_dot_tn(jnp.concatenate([bh[g], kh[g]], axis=0), jnp.concatenate([u_loc[g], v[g]], axis=0)))
              for g in groups]
        q = [rt[g] + bdmm(a_rb[g], ta[g]) for g in groups]
        y_loc = [bdmm(a_rb[g], u_loc[g]) + av[g][L:] for g in groups]

        m0 = [state_ref[g] for g in groups]
        qg = [mm(jnp.concatenate([q[g], gb[g]], axis=0), bd(m0[g])) for g in groups]
        y_ref[0, 0, rows, :] = jnp.concatenate([qg[g][:L] + y_loc[g] for g in groups], axis=1).astype(y_ref.dtype)
        e_diag = [_split2(jnp.where(eye, jnp.broadcast_to(jnp.exp(tot[g]), (L, GW)), 0.0)) for g in groups]
        w_hi = seg_sum([x[0] for x in e_diag])
        w_lo = seg_sum([x[1] for x in e_diag])
        for g in groups:
            state_ref[g] = (w_hi[g] + w_lo[g]) * m0[g] + qg[g][L:] + hm[g]

    u = -(_dot(hw_ref[0, 0], w2_ref[0]) + w0_ref[0])
    pre_ref[0] = -jnp.exp(-(jnp.maximum(u, 0.0) + jnp.log(1.0 + jnp.exp(-jnp.abs(u)))) - 0.5)
    pre_ref[1] = 1.0 / (1.0 + jnp.exp(-(_dot(ha_ref[0, 0], a2_ref[0]) + a0_ref[0])))

    n_sub = r_ref.shape[1] // L
    for sub in range(n_sub):
        chunk(pl.ds(pl.multiple_of(jnp.where(d == 0, sub, n_sub - 1 - sub) * L, L), L))


def wkv_bidir(r, k, v, hw, ha, w2, a2, w0, a0, k_k, k_a, r_k, *, n_groups=16):
    bsz, seq, dm = r.shape
    rank = hw.shape[-1]
    assert CHUNK == HEAD and seq % CHUNK == 0
    n_groups = min(n_groups, dm // GW)
    w = n_groups * GW
    assert dm % w == 0
    rows = WKV_STEP_CHUNKS * CHUNK if seq % (WKV_STEP_CHUNKS * CHUNK) == 0 else CHUNK
    nc = seq // rows

    def cidx(d, c):
        return c + d * (nc - 1 - 2 * c)

    x_spec = pl.BlockSpec((1, rows, w), lambda d, b, j, c: (b, cidx(d, c), j))
    d_spec = pl.BlockSpec((1, 1, rows, w), lambda d, b, j, c: (d, b, cidx(d, c), j))
    p_spec = pl.BlockSpec((1, w), lambda d, b, j, c: (0, j))
    h_spec = pl.BlockSpec((1, 1, rows, rank), lambda d, b, j, c: (d, b, cidx(d, c), 0))
    w2_spec = pl.BlockSpec((1, rank, w), lambda d, b, j, c: (d, 0, j))
    b_spec = pl.BlockSpec((1, 1, w), lambda d, b, j, c: (d, 0, j))
    out_sd = jax.ShapeDtypeStruct((2, bsz, seq, dm), BF16)
    return pl.pallas_call(
        functools.partial(_wkv_kernel, n_groups=n_groups),
        grid=(2, bsz, dm // w, nc),
        in_specs=[x_spec, x_spec, x_spec, h_spec, h_spec, w2_spec, w2_spec, b_spec, b_spec,
                  p_spec, p_spec, p_spec],
        out_specs=[d_spec, d_spec],
        out_shape=[out_sd, out_sd],
        scratch_shapes=[pltpu.VMEM((n_groups, HEAD, GW), F32), pltpu.VMEM((2, rows, w), F32)],
        compiler_params=_cparams(("parallel", "parallel", "parallel", "arbitrary")),
        name="wkv_bidir",
    )(r, k, v, hw, ha, w2, a2, w0.reshape(2, 1, dm), a0.reshape(2, 1, dm),
      k_k.reshape(1, dm), k_a.reshape(1, dm), r_k.reshape(1, dm))


def _rms(x, g):
    return x * lax.rsqrt(jnp.mean(x * x, axis=-1, keepdims=True) + RMS_EPS) * g


def _sigmoid(x):
    return 1.0 / (1.0 + jnp.exp(-x))


def _rmsnorm_kernel(x_ref, g_ref, o_ref):
    o_ref[...] = _rms(x_ref[...], g_ref[...]).astype(o_ref.dtype)


def rmsnorm(x, g, *, out_dtype, tm=512):
    m, dm = x.shape
    tm = min(tm, m)
    return pl.pallas_call(
        _rmsnorm_kernel,
        grid=(m // tm,),
        in_specs=[pl.BlockSpec((tm, dm), lambda i: (i, 0)), pl.BlockSpec((1, dm), lambda i: (0, 0))],
        out_specs=pl.BlockSpec((tm, dm), lambda i: (i, 0)),
        out_shape=jax.ShapeDtypeStruct((m, dm), out_dtype),
        compiler_params=_cparams(("parallel",)),
        name="rmsnorm",
    )(x, g.reshape(1, dm))


def _rwkv_pre_kernel(x_ref, xp_ref, xn_ref, g_ref, mu_ref, *o_refs, ts):
    i = pl.program_id(1)
    g = g_ref[...]
    hn = _rms(x_ref[0], g)
    h_before = jnp.where(i == 0, 0.0, _rms(xp_ref[0], g)[7:8])
    h_after = jnp.where(i == pl.num_programs(1) - 1, 0.0, _rms(xn_ref[0], g)[0:1])
    row = lax.broadcasted_iota(jnp.int32, hn.shape, 0)
    prev = jnp.where(row == 0, h_before, pltpu.roll(hn, 1, 0))
    nxt = jnp.where(row == ts - 1, h_after, pltpu.roll(hn, ts - 1, 0))
    delta = 0.5 * (prev + nxt) - hn
    for n, o_ref in enumerate(o_refs):
        o_ref[0] = (hn + delta * mu_ref[n:n + 1, :]).astype(o_ref.dtype)


def rwkv_pre(h, g, mu, *, ts=256):
    bsz, seq, dm = h.shape
    ts = min(ts, seq)
    n_mix = mu.shape[0]
    sub = 8
    nsub = ts // sub
    x_spec = pl.BlockSpec((1, ts, dm), lambda b, i: (b, i, 0))
    return pl.pallas_call(
        functools.partial(_rwkv_pre_kernel, ts=ts),
        grid=(bsz, seq // ts),
        in_specs=[x_spec,
                  pl.BlockSpec((1, sub, dm), lambda b, i: (b, jnp.maximum(i * nsub - 1, 0), 0)),
                  pl.BlockSpec((1, sub, dm), lambda b, i: (b, jnp.minimum((i + 1) * nsub, seq // sub - 1), 0)),
                  pl.BlockSpec((1, dm), lambda b, i: (0, 0)),
                  pl.BlockSpec((n_mix, dm), lambda b, i: (0, 0))],
        out_specs=[x_spec] * n_mix,
        out_shape=[jax.ShapeDtypeStruct((bsz, seq, dm), BF16)] * n_mix,
        compiler_params=_cparams(("parallel", "parallel")),
        name="rwkv_pre",
    )(h, h, h, g.reshape(1, dm), mu)


def _lora_kernel(x_ref, w1_ref, o_ref, *, act):
    hid = _dot(x_ref[...], w1_ref[0])
    if act == "tanh":
        hid = jnp.tanh(hid)
    elif act == "sigmoid":
        hid = _sigmoid(hid)
    o_ref[0] = hid.astype(o_ref.dtype)


def _pad_rank(w1, w2):
    rpad = -w1.shape[-1] % LANES
    return (jnp.pad(w1, ((0, 0), (0, 0), (0, rpad))).astype(BF16),
            jnp.pad(w2, ((0, 0), (0, rpad), (0, 0))).astype(BF16))


def lora_hidden(x, w1, *, act, tm=1024):
    m, dm = x.shape
    nd, _, rp = w1.shape
    tm = min(tm, m)
    return pl.pallas_call(
        functools.partial(_lora_kernel, act=act),
        grid=(nd, m // tm),
        in_specs=[pl.BlockSpec((tm, dm), lambda d, i: (i, 0)),
                  pl.BlockSpec((1, dm, rp), lambda d, i: (d, 0, 0))],
        out_specs=pl.BlockSpec((1, tm, rp), lambda d, i: (d, i, 0)),
        out_shape=jax.ShapeDtypeStruct((nd, m, rp), BF16),
        compiler_params=_cparams(("parallel", "parallel")),
        name="lora_" + act,
    )(x, w1)


def _rwkv_post_kernel(y_ref, b_ref, hg_ref, g2_ref, lw_ref, lb_ref, o_ref):
    dm = o_ref.shape[-1]
    seg = (lax.broadcasted_iota(jnp.int32, (GW, GW), 0) // HEAD
           == lax.broadcasted_iota(jnp.int32, (GW, GW), 1) // HEAD).astype(BF16)
    hg = hg_ref[0]
    for s in range(dm // GW):
        sl = slice(s * GW, (s + 1) * GW)
        y = y_ref[0, :, sl].astype(F32) + y_ref[1, :, sl].astype(F32)
        mean = _dot(y, seg) * (1.0 / HEAD)
        yc = y - mean
        var = _dot(yc * yc, seg) * (1.0 / HEAD)
        yn = yc * lax.rsqrt(var + GN_EPS) * lw_ref[:, sl] + lb_ref[:, sl]
        out = yn + (b_ref[0, :, sl].astype(F32) + b_ref[1, :, sl].astype(F32))
        gate = _dot(hg, g2_ref[:, sl])
        o_ref[:, sl] = (out * gate).astype(o_ref.dtype)


def rwkv_post(y, bonus, hg, g2, ln_w, ln_b, *, tm=512):
    _, m, dm = y.shape
    rank = hg.shape[-1]
    tm = min(tm, m)
    yspec = pl.BlockSpec((2, tm, dm), lambda i: (0, i, 0))
    pspec = pl.BlockSpec((1, dm), lambda i: (0, 0))
    return pl.pallas_call(
        _rwkv_post_kernel,
        grid=(m // tm,),
        in_specs=[yspec, yspec, pl.BlockSpec((1, tm, rank), lambda i: (0, i, 0)),
                  pl.BlockSpec((rank, dm), lambda i: (0, 0)), pspec, pspec],
        out_specs=pl.BlockSpec((tm, dm), lambda i: (i, 0)),
        out_shape=jax.ShapeDtypeStruct((m, dm), BF16),
        compiler_params=_cparams(("parallel",)),
        name="rwkv_post",
    )(y, bonus, hg, g2, ln_w.reshape(1, dm), ln_b.reshape(1, dm))


def _ffn_kernel(be_ref, nu_ref, nq_ref, x_ref, wg_ref, wu_ref, wd_ref, *rest, has_res, has_norm):
    del be_ref, nu_ref
    tm = x_ref.shape[0]
    res_ref = rest[0] if has_res else None
    g_ref = rest[int(has_res)] if has_norm else None
    o_ref = rest[-2] if has_norm else rest[-1]
    f = pl.program_id(1)

    @pl.when(f == 0)
    def _():
        o_ref[...] = res_ref[...] if has_res else jnp.zeros_like(o_ref)

    n_quarters = nq_ref[pl.program_id(0)]
    for q in range(1, ROW_SPLIT + 1):
        rows = q * tm // ROW_SPLIT

        @pl.when(n_quarters == q)
        def _(rows=rows):
            x = x_ref[:rows].astype(BF16)
            gate = _dot(x, wg_ref[0])
            up = _dot(x, wu_ref[0])
            o_ref[:rows] += _dot(gate * _sigmoid(gate) * up, wd_ref[0])

    if has_norm:
        @pl.when(f == pl.num_programs(1) - 1)
        def _():
            rest[-1][...] = _rms(o_ref[...], g_ref[...]).astype(rest[-1].dtype)


def ffn(x, blk_e, n_used, n_quarters, wg, wu, wd, *, res=None, norm_g=None, tm, tf):
    p, dm = x.shape
    fdim = wg.shape[-1]
    assert p % tm == 0 and fdim % tf == 0 and tm % (ROW_SPLIT * 16) == 0
    nf = fdim // tf

    def blk(i, nu):
        return jnp.minimum(i, nu[0] - 1)

    def fblk(i, f, nu):
        return jnp.where(i < nu[0], f, nf - 1)

    row_spec = pl.BlockSpec((tm, dm), lambda i, f, be, nu, nq: (i, 0))
    in_specs = [pl.BlockSpec((tm, dm), lambda i, f, be, nu, nq: (blk(i, nu), 0)),
                pl.BlockSpec((1, dm, tf), lambda i, f, be, nu, nq: (be[blk(i, nu)], 0, fblk(i, f, nu))),
                pl.BlockSpec((1, dm, tf), lambda i, f, be, nu, nq: (be[blk(i, nu)], 0, fblk(i, f, nu))),
                pl.BlockSpec((1, tf, dm), lambda i, f, be, nu, nq: (be[blk(i, nu)], fblk(i, f, nu), 0))]
    args = [x, wg, wu, wd]
    if res is not None:
        in_specs.append(row_spec)
        args.append(res)
    out_specs, out_shape = row_spec, jax.ShapeDtypeStruct((p, dm), F32)
    if norm_g is not None:
        in_specs.append(pl.BlockSpec((1, dm), lambda i, f, be, nu, nq: (0, 0)))
        args.append(norm_g.reshape(1, dm))
        out_specs, out_shape = [row_spec, row_spec], [out_shape, jax.ShapeDtypeStruct((p, dm), BF16)]
    return pl.pallas_call(
        functools.partial(_ffn_kernel, has_res=res is not None, has_norm=norm_g is not None),
        grid_spec=pltpu.PrefetchScalarGridSpec(
            num_scalar_prefetch=3, grid=(p // tm, nf),
            in_specs=in_specs, out_specs=out_specs),
        out_shape=out_shape,
        compiler_params=_cparams(("parallel", "arbitrary")),
        name="ffn",
    )(blk_e, n_used, n_quarters, *args)


def _rope_table_kernel(pos_ref, inv_ref, cos_ref, sin_ref):
    ang = pos_ref[...].astype(F32) * inv_ref[...]
    lane = lax.broadcasted_iota(jnp.int32, ang.shape, 1)
    cos_ref[...] = jnp.cos(ang)
    sin_ref[...] = jnp.where((lane % HEAD) < HEAD // 2, -jnp.sin(ang), jnp.sin(ang))


def rope_table(positions, *, tm=1024):
    m = positions.shape[0]
    tm = min(tm, m)
    half = HEAD // 2
    inv = ROPE_THETA ** (-(2.0 * (jnp.arange(LANES) % half)).astype(F32) / HEAD)
    spec = pl.BlockSpec((tm, LANES), lambda i: (i, 0))
    return pl.pallas_call(
        _rope_table_kernel,
        grid=(m // tm,),
        in_specs=[pl.BlockSpec((tm, 1), lambda i: (i, 0)), pl.BlockSpec((1, LANES), lambda i: (0, 0))],
        out_specs=[spec, spec],
        out_shape=[jax.ShapeDtypeStruct((m, LANES), F32)] * 2,
        compiler_params=_cparams(("parallel",)),
        name="rope_table",
    )(positions.reshape(m, 1), inv.reshape(1, LANES))


def _proj_rope_kernel(a_ref, b_ref, cos_ref, sin_ref, o_ref, *, scale, n_sub):
    tm, tn = o_ref.shape
    sub = tm // n_sub
    lane = lax.broadcasted_iota(jnp.int32, (sub, LANES), 1)
    first = (lane % HEAD) < HEAD // 2

    def project(i):
        return _dot(a_ref[i * sub:(i + 1) * sub, :], b_ref[...])

    def rotate(i, acc):
        rows = slice(i * sub, (i + 1) * sub)
        cos = cos_ref[rows, :]
        sin_signed = sin_ref[rows, :]
        for s in range(tn // LANES):
            sl = slice(s * LANES, (s + 1) * LANES)
            t = acc[:, sl]
            partner = jnp.where(first, pltpu.roll(t, LANES - HEAD // 2, 1), pltpu.roll(t, HEAD // 2, 1))
            o_ref[rows, sl] = ((t * cos + partner * sin_signed) * scale).astype(o_ref.dtype)

    acc = project(0)
    for i in range(1, n_sub):
        nxt = project(i)
        rotate(i - 1, acc)
        acc = nxt
    rotate(n_sub - 1, acc)


def _proj_ones_kernel(a_ref, b_ref, o_ref):
    acc = _dot(a_ref[...], b_ref[...])
    lane = lax.broadcasted_iota(jnp.int32, (acc.shape[0], LANES), 1)
    ones_col = jnp.where(lane == 0, 1.0, 0.0).astype(o_ref.dtype)
    for s in range(acc.shape[1] // LANES):
        o_ref[:, 2 * s * LANES:(2 * s + 1) * LANES] = acc[:, s * LANES:(s + 1) * LANES].astype(o_ref.dtype)
        o_ref[:, (2 * s + 1) * LANES:(2 * s + 2) * LANES] = ones_col


def qkv_rope(hn, w_qkv, cos, sin_signed, *, tm=1024, tn=512):
    m, dm = hn.shape
    assert 2 * HEAD == LANES
    tm, tn = min(tm, m), min(tn, dm)
    nb = dm // tn
    a_spec = pl.BlockSpec((tm, dm), lambda i, j: (i, 0))
    t_spec = pl.BlockSpec((tm, LANES), lambda i, j: (i, 0))

    def b_spec(part):
        return pl.BlockSpec((dm, tn), lambda i, j: (0, j + part * nb))

    def rope_proj(part, scale):
        return pl.pallas_call(
            functools.partial(_proj_rope_kernel, scale=scale, n_sub=4 if tm % 64 == 0 else 1),
            grid=(m // tm, nb),
            in_specs=[a_spec, b_spec(part), t_spec, t_spec],
            out_specs=pl.BlockSpec((tm, tn), lambda i, j: (i, j)),
            out_shape=jax.ShapeDtypeStruct((m, dm), BF16),
            compiler_params=_cparams(("parallel", "parallel")),
            name="proj_rope",
        )(hn, w_qkv, cos, sin_signed)

    q = rope_proj(0, HEAD ** -0.5 * LOG2E)
    k = rope_proj(1, 1.0)
    v1 = pl.pallas_call(
        _proj_ones_kernel,
        grid=(m // tm, nb),
        in_specs=[a_spec, b_spec(2)],
        out_specs=pl.BlockSpec((tm, 2 * tn), lambda i, j: (i, j)),
        out_shape=jax.ShapeDtypeStruct((m, 2 * dm), BF16),
        compiler_params=_cparams(("parallel", "parallel")),
        name="proj_ones",
    )(hn, w_qkv)
    return q, k, v1


def _attn_kernel(q_ref, k_ref, v_ref, lq1_ref, lk1_ref, lq2_ref, lk2_ref, sub_ref, o_ref,
                 m_ref, acc_ref, s_ref, *, lambda_init, tk):
    n_slots, _, tq, hw = m_ref.shape
    n_tiles = q_ref.shape[0] // tq
    seq = k_ref.shape[0]
    n_chunks = seq // tk
    lam = (jnp.exp(jnp.sum(lq1_ref[...] * lk1_ref[...], keepdims=True))
           - jnp.exp(jnp.sum(lq2_ref[...] * lk2_ref[...], keepdims=True)) + lambda_init)

    def rows(j):
        return slice(j * tk, (j + 1) * tk)

    for tile in range(n_tiles):
        t = tile % n_slots
        q = q_ref[tile * tq:(tile + 1) * tq, :]
        lane = lax.broadcasted_iota(jnp.int32, q.shape, 1)
        qs = [jnp.where((lane >= c * HEAD) & (lane < (c + 1) * HEAD), q, jnp.zeros((), q.dtype))
              for c in range(2)]

        def scores(j, slot, t=t, qs=qs):
            ks = k_ref[rows(j), :]
            for c in range(2):
                s_ref[t, slot, c] = _dot_nt(qs[c], ks)

        def accumulate(j, slot, first=False, t=t):
            vs = v_ref[rows(j), :]
            for c in range(2):
                s = s_ref[t, slot, c]
                row_max = jnp.max(s, axis=1, keepdims=True)
                if first:
                    m_new = jnp.broadcast_to(row_max, (tq, hw))
                    acc_ref[t, c] = _dot(jnp.exp2(s - row_max), vs)
                else:
                    m_prev = m_ref[t, c]
                    m_new = jnp.maximum(m_prev, row_max)
                    alpha = jnp.exp2(m_prev - m_new)
                    p = jnp.exp2(s - jnp.concatenate([m_new] * (tk // hw), axis=1))
                    acc_ref[t, c] = acc_ref[t, c] * jnp.concatenate([alpha, alpha], axis=1) + _dot(p, vs)
                m_ref[t, c] = m_new

        scores(0, 0)
        scores(1, 1)
        accumulate(0, 0, first=True)
        for j in range(2, n_chunks):
            scores(j, j % 2)
            accumulate(j - 1, (j - 1) % 2)
        accumulate(n_chunks - 1, (n_chunks - 1) % 2)
        a1 = acc_ref[t, 0]
        a2 = acc_ref[t, 1]
        o = a1[:, :hw] / a1[:, hw:hw + 1] - lam * (a2[:, :hw] / a2[:, hw:hw + 1])
        o = o * lax.rsqrt(jnp.mean(o * o, axis=-1, keepdims=True) + SUBLN_EPS) * sub_ref[...] * (1.0 - lambda_init)
        o_ref[tile * tq:(tile + 1) * tq, :] = o.astype(o_ref.dtype)


def diff_attn(q, k, v1, lq1, lk1, lq2, lk2, subln, lambda_init, *, bsz, tq=512, tk=2048):
    m, dm = q.shape
    seq = m // bsz
    hw = 2 * HEAD
    tq, tk = min(tq, seq), min(tk, seq // 2)
    n_tiles = ATTN_STEP_TILES if seq % (ATTN_STEP_TILES * tq) == 0 else 1
    assert seq % (n_tiles * tq) == 0 and seq % (2 * tk) == 0
    nq = seq // (n_tiles * tq)
    n_slots = min(n_tiles, 2)
    q_spec = pl.BlockSpec((n_tiles * tq, hw), lambda b, h, i: (b * nq + i, h))
    l_spec = pl.BlockSpec((1, HEAD), lambda b, h, i: (0, 0))
    return pl.pallas_call(
        functools.partial(_attn_kernel, lambda_init=lambda_init, tk=tk),
        grid=(bsz, dm // hw, nq),
        in_specs=[q_spec, pl.BlockSpec((seq, hw), lambda b, h, i: (b, h)),
                  pl.BlockSpec((seq, 2 * hw), lambda b, h, i: (b, h)), l_spec, l_spec, l_spec, l_spec,
                  pl.BlockSpec((1, hw), lambda b, h, i: (0, 0))],
        out_specs=q_spec,
        out_shape=jax.ShapeDtypeStruct((m, dm), BF16),
        scratch_shapes=[pltpu.VMEM((n_slots, 2, tq, hw), F32), pltpu.VMEM((n_slots, 2, tq, 2 * hw), F32),
                        pltpu.VMEM((n_slots, 2, 2, tq, tk), F32)],
        compiler_params=_cparams(("parallel", "parallel", "parallel")),
        name="diff_attn",
    )(q, k, v1, lq1.reshape(1, HEAD), lk1.reshape(1, HEAD), lq2.reshape(1, HEAD), lk2.reshape(1, HEAD),
      subln.reshape(1, hw))


def _router_kernel(h_ref, g_ref, wr_ref, hn_ref, e_ref, gate_ref, rank_ref, cnt_ref, *, n_exp):
    tm = h_ref.shape[0]

    @pl.when(pl.program_id(0) == 0)
    def _():
        cnt_ref[...] = jnp.zeros_like(cnt_ref)

    hn = _rms(h_ref[...], g_ref[...])
    hn_ref[...] = hn
    lane = lax.broadcasted_iota(jnp.int32, (tm, LANES), 1)
    logits = jnp.where(lane < n_exp, _dot_f32(hn, wr_ref[...]), -jnp.inf)
    m1 = jnp.max(logits, axis=1, keepdims=True)
    i1 = jnp.min(jnp.where(logits == m1, lane, LANES), axis=1, keepdims=True)
    rest = jnp.where(lane == i1, -jnp.inf, logits)
    m2 = jnp.max(rest, axis=1, keepdims=True)
    i2 = jnp.min(jnp.where(rest == m2, lane, LANES), axis=1, keepdims=True)
    e21 = jnp.exp(m2 - m1)
    g1 = 1.0 / (1.0 + e21)
    g2 = e21 / (1.0 + e21)
    oh1 = lane == i1
    oh2 = lane == i2
    both = jnp.where(oh1 | oh2, 1.0, 0.0)
    earlier = (lax.broadcasted_iota(jnp.int32, (tm, tm), 0) > lax.broadcasted_iota(jnp.int32, (tm, tm), 1))
    cum = _dot(earlier.astype(BF16), both) + cnt_ref[...]
    r1 = jnp.sum(jnp.where(oh1, cum, 0.0), axis=1, keepdims=True)
    r2 = jnp.sum(jnp.where(oh2, cum, 0.0), axis=1, keepdims=True)
    cnt_ref[...] += jnp.sum(both, axis=0, keepdims=True)
    e_ref[...] = jnp.where(lane == 0, i1, jnp.where(lane == 1, i2, 0))
    gate_ref[...] = jnp.where(lane == 0, g1, jnp.where(lane == 1, g2, 0.0))
    rank_ref[...] = jnp.where(lane == 0, r1, jnp.where(lane == 1, r2, 0.0)).astype(jnp.int32)


def moe_route(h, g, w_router, *, tm=512):
    m, dm = h.shape
    n_exp = w_router.shape[1]
    tm = min(tm, m)
    wr = jnp.pad(w_router, ((0, 0), (0, LANES - n_exp)))
    row = lambda dt: jax.ShapeDtypeStruct((m, LANES), dt)
    lspec = pl.BlockSpec((tm, LANES), lambda i: (i, 0))
    return pl.pallas_call(
        functools.partial(_router_kernel, n_exp=n_exp),
        grid=(m // tm,),
        in_specs=[pl.BlockSpec((tm, dm), lambda i: (i, 0)), pl.BlockSpec((1, dm), lambda i: (0, 0)),
                  pl.BlockSpec((dm, LANES), lambda i: (0, 0))],
        out_specs=[pl.BlockSpec((tm, dm), lambda i: (i, 0)), lspec, lspec, lspec,
                   pl.BlockSpec((1, LANES), lambda i: (0, 0))],
        out_shape=[jax.ShapeDtypeStruct((m, dm), F32), row(jnp.int32), row(F32), row(jnp.int32),
                   jax.ShapeDtypeStruct((1, LANES), F32)],
        compiler_params=_cparams(("arbitrary",)),
        name="moe_route",
    )(h, g.reshape(1, dm), wr)


def _dest_kernel(e_ref, rank_ref, cnt_ref, dest_ref, be_ref, *, n_exp, bm):
    tm = e_ref.shape[0]
    nb = be_ref.shape[0]
    cnt = cnt_ref[...]
    pcnt = jnp.floor((cnt + (bm - 1)) * (1.0 / bm)) * bm
    upper = (lax.broadcasted_iota(jnp.int32, (LANES, LANES), 0)
             <= lax.broadcasted_iota(jnp.int32, (LANES, LANES), 1)).astype(F32)
    pends = _dot_f32(jnp.broadcast_to(pcnt, (8, LANES)), upper)[0:1]
    pstart = pends - pcnt
    lane = lax.broadcasted_iota(jnp.int32, (tm, LANES), 1)
    e = e_ref[...]
    d1 = jnp.sum(jnp.where(lane == e[:, 0:1], pstart, 0.0), axis=1, keepdims=True)
    d2 = jnp.sum(jnp.where(lane == e[:, 1:2], pstart, 0.0), axis=1, keepdims=True)
    dest = jnp.where(lane == 0, d1, jnp.where(lane == 1, d2, 0.0)).astype(jnp.int32) + rank_ref[...]
    dest_ref[...] = dest
    blk_lane = lax.broadcasted_iota(jnp.int32, (nb, LANES), 1)
    blk_row = lax.broadcasted_iota(jnp.int32, (nb, LANES), 0).astype(F32) * bm
    ended = jnp.where((blk_lane < n_exp) & (pends <= blk_row), 1.0, 0.0)
    be = jnp.minimum(jnp.sum(ended, axis=1, keepdims=True), n_exp - 1.0)
    n_used = jnp.sum(jnp.where(blk_lane == n_exp - 1, pends, 0.0), axis=1, keepdims=True) * (1.0 / bm)
    seg_end = jnp.sum(jnp.where(blk_lane.astype(F32) == be, pstart + cnt, 0.0), axis=1, keepdims=True)
    valid = jnp.clip(seg_end - blk_row[:, 0:1], 0.0, float(bm))
    quarters = jnp.ceil(valid * (ROW_SPLIT / bm))
    be_ref[...] = jnp.where(blk_lane == 1, n_used, jnp.where(blk_lane == 2, quarters, be)).astype(jnp.int32)


def moe_dest(e, rank, cnt, *, n_exp, bm, nblk, tm=512):
    m = e.shape[0]
    tm = min(tm, m)
    nb = -(-nblk // 8) * 8
    lspec = pl.BlockSpec((tm, LANES), lambda i: (i, 0))
    return pl.pallas_call(
        functools.partial(_dest_kernel, n_exp=n_exp, bm=bm),
        grid=(m // tm,),
        in_specs=[lspec, lspec, pl.BlockSpec((1, LANES), lambda i: (0, 0))],
        out_specs=[lspec, pl.BlockSpec((nb, LANES), lambda i: (0, 0))],
        out_shape=[jax.ShapeDtypeStruct((m, LANES), jnp.int32), jax.ShapeDtypeStruct((nb, LANES), jnp.int32)],
        compiler_params=_cparams(("arbitrary",)),
        name="moe_dest",
    )(e, rank, cnt)


def _row_copy(src, dst, sem):
    return pltpu.make_async_copy(src, dst, sem)


def _dispatch_kernel(dest_ref, hn_ref, xs_in_ref, xs_ref, sem):
    del xs_in_ref
    tm = hn_ref.shape[0]
    base = pl.program_id(0) * tm

    def issue(r, carry):
        for s in range(TOP_K):
            dst = dest_ref[TOP_K * (base + r) + s]
            _row_copy(hn_ref.at[pl.ds(r, 1)], xs_ref.at[pl.ds(dst, 1)], sem).start()
        return carry

    lax.fori_loop(0, tm, issue, 0)

    def drain(r, carry):
        for s in range(TOP_K):
            _row_copy(hn_ref.at[pl.ds(0, 1)], xs_ref.at[pl.ds(0, 1)], sem).wait()
        return carry

    lax.fori_loop(0, tm, drain, 0)


def moe_dispatch(dest_flat, hn, n_rows, *, tm=512):
    m, dm = hn.shape
    tm = min(tm, m)
    xs0 = jnp.zeros((n_rows, dm), hn.dtype)
    return pl.pallas_call(
        _dispatch_kernel,
        grid_spec=pltpu.PrefetchScalarGridSpec(
            num_scalar_prefetch=1, grid=(m // tm,),
            in_specs=[pl.BlockSpec((tm, dm), lambda i, d: (i, 0)), pl.BlockSpec(memory_space=pl.ANY)],
            out_specs=pl.BlockSpec(memory_space=pl.ANY),
            scratch_shapes=[pltpu.SemaphoreType.DMA(())]),
        out_shape=jax.ShapeDtypeStruct((n_rows, dm), hn.dtype),
        input_output_aliases={2: 0},
        compiler_params=_cparams(("arbitrary",)),
        name="moe_dispatch",
    )(dest_flat, hn, xs0)


def _combine_kernel(dest_ref, yb_ref, gate_ref, res_ref, *rest, has_norm):
    norm_ref = rest[0] if has_norm else None
    o_ref, buf_ref, sem = rest[-3:]
    tm = res_ref.shape[0]
    base = pl.program_id(0) * tm

    def issue(r, carry):
        for s in range(TOP_K):
            src = dest_ref[TOP_K * (base + r) + s]
            _row_copy(yb_ref.at[pl.ds(src, 1)], buf_ref.at[s, pl.ds(r, 1)], sem).start()
        return carry

    lax.fori_loop(0, tm, issue, 0)

    def drain(r, carry):
        for s in range(TOP_K):
            _row_copy(yb_ref.at[pl.ds(0, 1)], buf_ref.at[s, pl.ds(0, 1)], sem).wait()
        return carry

    lax.fori_loop(0, tm, drain, 0)
    gate = gate_ref[...]
    out = res_ref[...] + (gate[:, 0:1] * buf_ref[0] + gate[:, 1:2] * buf_ref[1])
    o_ref[...] = _rms(out, norm_ref[...]) if norm_ref is not None else out


def moe_combine(dest_flat, yb, gates, res, *, norm_g=None, tm=512):
    m, dm = res.shape
    tm = min(tm, m)
    row_spec = pl.BlockSpec((tm, dm), lambda i, d: (i, 0))
    in_specs = [pl.BlockSpec(memory_space=pl.ANY), pl.BlockSpec((tm, LANES), lambda i, d: (i, 0)), row_spec]
    args = [dest_flat, yb, gates, res]
    if norm_g is not None:
        in_specs.append(pl.BlockSpec((1, dm), lambda i, d: (0, 0)))
        args.append(norm_g.reshape(1, dm))
    return pl.pallas_call(
        functools.partial(_combine_kernel, has_norm=norm_g is not None),
        grid_spec=pltpu.PrefetchScalarGridSpec(
            num_scalar_prefetch=1, grid=(m // tm,),
            in_specs=in_specs,
            out_specs=row_spec,
            scratch_shapes=[pltpu.VMEM((TOP_K, tm, dm), F32), pltpu.SemaphoreType.DMA(())]),
        out_shape=jax.ShapeDtypeStruct((m, dm), F32),
        compiler_params=_cparams(("arbitrary",)),
        name="moe_combine",
    )(*args)


def moe_layer(h, g, w_router, wg, wu, wd, *, norm_g=None):
    m, dm = h.shape
    n_exp = w_router.shape[1]
    bm = 1024 if m * TOP_K >= 8 * 1024 else 128
    n_rows = m * TOP_K + n_exp * bm
    nblk = n_rows // bm
    hn, e, gates, rank, cnt = moe_route(h, g, w_router)
    dest, be = moe_dest(e, rank, cnt, n_exp=n_exp, bm=bm, nblk=nblk)
    dest_flat = dest[:, :TOP_K].reshape(-1)
    xs = moe_dispatch(dest_flat, hn, n_rows)
    tf = 256 if wg.shape[-1] % 256 == 0 else LANES
    yb = ffn(xs, be[:nblk, 0], be[0, 1:2], be[:nblk, 2], wg, wu, wd, tm=bm, tf=tf)
    return moe_combine(dest_flat, yb, gates, h, norm_g=norm_g)


def rwkv_layer(h, bsz, g_norm, mu, w_rkv, g1, g2, w0, w1, w2, a0, a1, a2, k_k, k_a, r_k, ln_w, ln_b, w_o,
               g_next):
    m, dm = h.shape
    seq = m // bsz
    xr, xk, xv, xw, xa, xg = (t.reshape(m, dm) for t in rwkv_pre(h.reshape(bsz, seq, dm), g_norm, mu))
    w_rkv = w_rkv.astype(BF16)
    r, k, v = (matmul(xi, w_rkv, n=dm, col_off=n * dm, out_dtype=BF16).reshape(bsz, seq, dm)
               for n, xi in enumerate((xr, xk, xv)))
    g1, g2 = _pad_rank(g1[None], g2[None])
    w1, w2 = _pad_rank(w1, w2)
    a1, a2 = _pad_rank(a1, a2)
    hg = lora_hidden(xg, g1, act="sigmoid")
    hw = lora_hidden(xw, w1, act="tanh").reshape(2, bsz, seq, -1)
    ha = lora_hidden(xa, a1, act="none").reshape(2, bsz, seq, -1)
    y, bonus = wkv_bidir(r, k, v, hw, ha, w2, a2, w0, a0, k_k, k_a, r_k)
    z = rwkv_post(y.reshape(2, m, dm), bonus.reshape(2, m, dm), hg, g2[0], ln_w, ln_b)
    return matmul(z, w_o.astype(BF16), res=h, norm_g=g_next)


def attn_layer(h, hn, bsz, positions, w_qkv, lq1, lk1, lq2, lk2, subln, w_o, lambda_init):
    cos, sin_signed = rope_table(positions.reshape(-1))
    q, k, v1 = qkv_rope(hn, w_qkv.astype(BF16), cos, sin_signed)
    o = diff_attn(q, k, v1, lq1, lk1, lq2, lk2, subln, lambda_init, bsz=bsz)
    return matmul(o, w_o.astype(BF16), res=h)


def dense_ffn_layer(h, hn, wg, wu, wd, g_next):
    m = h.shape[0]
    tm = min(512, m)
    tf = 1024 if wg.shape[-1] % 1024 == 0 else LANES
    nblk = m // tm
    return ffn(hn, jnp.zeros((nblk,), jnp.int32), jnp.full((1,), nblk, jnp.int32),
               jnp.full((nblk,), ROW_SPLIT, jnp.int32), wg.astype(BF16)[None], wu.astype(BF16)[None],
               wd.astype(BF16)[None], res=h, norm_g=g_next, tm=tm, tf=tf)


def kernel(x, positions, norm_mix, norm_ffn, norm_final, rw_mu, rw_w_rkv, rw_g1, rw_g2, rw_w0, rw_w1, rw_w2, rw_a0, rw_a1, rw_a2, rw_kk, rw_ka, rw_rk, rw_ln_w, rw_ln_b, rw_w_o, da_w_qkv, da_lq1, da_lk1, da_lq2, da_lk2, da_subln, da_w_o, ff_wg, ff_wu, ff_wd, moe_router, moe_wg, moe_wu, moe_wd):
    bsz, seq, dm = x.shape
    depth = norm_mix.shape[0]
    h = x.reshape(bsz * seq, dm)
    hn = None
    for i in range(depth):
        j = i // 2
        if i % 2 == 0:
            h, hn = rwkv_layer(h, bsz, norm_mix[i], rw_mu[j], rw_w_rkv[j], rw_g1[j], rw_g2[j], rw_w0[j], rw_w1[j],
                               rw_w2[j], rw_a0[j], rw_a1[j], rw_a2[j], rw_kk[j], rw_ka[j], rw_rk[j],
                               rw_ln_w[j], rw_ln_b[j], rw_w_o[j], norm_ffn[i])
            if i + 1 < depth:
                h, hn = dense_ffn_layer(h, hn, ff_wg[j], ff_wu[j], ff_wd[j], norm_mix[i + 1])
            else:
                h = dense_ffn_layer(h, hn, ff_wg[j], ff_wu[j], ff_wd[j], None)
        else:
            lambda_init = 0.8 - 0.6 * math.exp(-0.3 * i)
            h = attn_layer(h, hn, bsz, positions, da_w_qkv[j], da_lq1[j], da_lk1[j], da_lq2[j],
                           da_lk2[j], da_subln[j], da_w_o[j], lambda_init)
            last = i == depth - 1
            h = moe_layer(h, norm_ffn[i], moe_router[j], moe_wg[j], moe_wu[j], moe_wd[j],
                          norm_g=norm_final if last else None)
    if depth % 2 == 1 or depth == 0:
        h = rmsnorm(h, norm_final, out_dtype=x.dtype)
    return h.reshape(bsz, seq, dm)
```
